```python
import math
import jax, jax.numpy as jnp
from jax import lax
import numpy as np

D_MODEL = 1024
BATCH = 8
SEQ = 8192
DEPTH = 1

N_MEM = 256
EPS = 1e-6

SSD_D_INNER = D_MODEL
SSD_HEAD_DIM = 64
SSD_HEADS = SSD_D_INNER // SSD_HEAD_DIM
SSD_GROUPS = 4
SSD_HEADS_PER_GROUP = SSD_HEADS // SSD_GROUPS
SSD_STATE = 128
SSD_CONV = 4
SSD_CHUNK = 128
SSD_CONV_DIM = SSD_D_INNER + 2 * SSD_GROUPS * SSD_STATE

DIL_PAIRS = ((128, 1), (512, 4), (2048, 16))
DIL_HEADS_PER_GROUP = 4
DIL_HEAD_DIM = 64
DIL_HEADS = len(DIL_PAIRS) * DIL_HEADS_PER_GROUP
DIL_WIDTH = DIL_HEADS * DIL_HEAD_DIM
DIL_OUT_WIDTH = DIL_HEADS_PER_GROUP * DIL_HEAD_DIM

MEM_HEADS = 4
MEM_HEAD_DIM = 192
MEM_WIDTH = MEM_HEADS * MEM_HEAD_DIM

ROPE_THETA = 500000.0
ROPE_FRACTION = 4

PEER_HEADS = 8
PEER_N_KEYS = 128
PEER_EXPERTS = PEER_N_KEYS * PEER_N_KEYS
PEER_QUERY_DIM = 256
PEER_TOPK = 16
PEER_BLOCK = 64

N_BRANCHES = 3

OFF_Z = 0
OFF_XBC = OFF_Z + SSD_D_INNER
OFF_DT = OFF_XBC + SSD_CONV_DIM
OFF_DQ = OFF_DT + SSD_HEADS
OFF_DK = OFF_DQ + DIL_WIDTH
OFF_DV = OFF_DK + DIL_WIDTH
OFF_MQ = OFF_DV + DIL_WIDTH
OFF_GATE = OFF_MQ + MEM_WIDTH
IN_PROJ_WIDTH = OFF_GATE + N_BRANCHES * D_MODEL

kernel_name = 'hybrid_ssd_dilated_memory_peer_block'

F32 = jnp.float32


def _rms_norm(t, w):
    tf = t.astype(F32)
    tf = tf * lax.rsqrt(jnp.mean(tf * tf, axis=-1, keepdims=True) + EPS)
    return (tf * w.astype(F32)).astype(t.dtype)


def _partial_rope(t, positions):
    hd = t.shape[-1]
    rot = hd // ROPE_FRACTION
    half = rot // 2
    inv = jnp.exp(-math.log(ROPE_THETA) * (2.0 / rot) * jnp.arange(half, dtype=F32))
    ang = positions.astype(F32)[..., None] * inv
    cos = jnp.cos(ang)[:, :, None, :]
    sin = jnp.sin(ang)[:, :, None, :]
    tf = t.astype(F32)
    t1 = tf[..., :half]
    t2 = tf[..., half:rot]
    out = jnp.concatenate([t1 * cos - t2 * sin, t2 * cos + t1 * sin, tf[..., rot:]], axis=-1)
    return out.astype(t.dtype)


def _causal_depthwise_conv(t, w, bias):
    c = t.shape[-1]
    y = lax.conv_general_dilated(
        t, w[:, None, :].astype(t.dtype), window_strides=(1,),
        padding=[(w.shape[0] - 1, 0)],
        dimension_numbers=('NWC', 'WIO', 'NWC'),
        feature_group_count=c)
    return y + bias.astype(t.dtype)


def _ssd_chunked(xh, dt, a, bmat, cmat):
    b, s, g, r, p = xh.shape
    n = bmat.shape[-1]
    nc = s // SSD_CHUNK
    x_dt = xh.astype(F32) * dt[..., None]
    da = dt * a

    def chunks(t):
        return jnp.moveaxis(t.reshape(b, nc, SSD_CHUNK, *t.shape[2:]), 1, 0)

    tril = jnp.tril(jnp.ones((SSD_CHUNK, SSD_CHUNK), dtype=bool))
    strict = jnp.tril(jnp.ones((SSD_CHUNK, SSD_CHUNK), dtype=bool), k=-1)

    def step(state, inp):
        xc, dac, bc, cc = inp
        dat = jnp.moveaxis(dac, 1, -1)
        acum = jnp.cumsum(dat, axis=-1)
        seg = jnp.cumsum(jnp.where(strict, dat[..., :, None], 0.0), axis=-2)
        lmat = jnp.exp(jnp.where(tril, seg, -jnp.inf))
        cb = jnp.einsum('bsgn,btgn->bgst', cc, bc)
        y_diag = jnp.einsum('bgst,bgrst,btgrp->bsgrp', cb, lmat, xc)
        y_off = jnp.einsum('bsgn,bgrpn,bgrs->bsgrp', cc, state, jnp.exp(acum))
        decay = jnp.exp(acum[..., -1:] - acum)
        new_state = (state * jnp.exp(acum[..., -1])[..., None, None]
                     + jnp.einsum('btgn,bgrt,btgrp->bgrpn', bc, decay, xc))
        return new_state, y_diag + y_off

    state0 = jnp.zeros((b, g, r, p, n), F32)
    _, ys = lax.scan(step, state0, (chunks(x_dt), chunks(da),
                                    chunks(bmat.astype(F32)), chunks(cmat.astype(F32))))
    return jnp.moveaxis(ys, 0, 1).reshape(b, s, g, r, p)


def _dilated_group_attention(q, k, v, dilation, n_back):
    b, s, h, hd = q.shape
    m = s // dilation
    blk = n_back
    nb = -(-m // blk)
    mp = nb * blk

    def to_sub(t):
        t = t.reshape(b, m, dilation, h, hd)
        t = jnp.pad(t, ((0, 0), (0, mp - m), (0, 0), (0, 0), (0, 0)))
        return t.reshape(b, nb, blk, dilation, h, hd)

    def with_prev(t):
        prev = jnp.pad(t[:, :-1], ((0, 0), (1, 0), (0, 0), (0, 0), (0, 0), (0, 0)))
        return jnp.concatenate([prev, t], axis=2)

    qs = to_sub(q)
    kk = with_prev(to_sub(k))
    vv = with_prev(to_sub(v))
    scores = jnp.einsum('bnqrhd,bnkrhd->bnrhqk', qs, kk).astype(F32) * (hd ** -0.5)
    qi = jnp.arange(blk)[:, None]
    ki = jnp.arange(2 * blk)[None, :]
    dist = qi + blk - ki
    band = (dist >= 0) & (dist <= n_back)
    valid = band[None] & ((jnp.arange(nb)[:, None, None] > 0) | (ki >= blk)[None])
    scores = jnp.where(valid[None, :, None, None], scores, -jnp.inf)
    lse = jax.nn.logsumexp(scores, axis=-1)
    probs = jnp.exp(scores - lse[..., None]).astype(v.dtype)
    out = jnp.einsum('bnrhqk,bnkrhd->bnqrhd', probs, vv)
    out = out.reshape(b, mp, dilation, h, hd)[:, :m].reshape(b, s, h, hd)
    lse = lse.transpose(0, 1, 4, 2, 3).reshape(b, mp, dilation, h)[:, :m].reshape(b, s, h)
    return out, lse


def _peer(hn, w_query, sub_keys, expert_down, expert_up):
    b, s, d = hn.shape
    q = (hn @ w_query).reshape(b, s, PEER_HEADS, 2, PEER_QUERY_DIM // 2)
    sc = jnp.einsum('bshcd,hckd->bshck', q, sub_keys).astype(F32)
    v1, i1 = lax.top_k(sc[..., 0, :], PEER_TOPK)
    v2, i2 = lax.top_k(sc[..., 1, :], PEER_TOPK)
    cand = (v1[..., :, None] + v2[..., None, :]).reshape(b, s, PEER_HEADS, PEER_TOPK * PEER_TOPK)
    cidx = (i1[..., :, None] * PEER_N_KEYS + i2[..., None, :]).reshape(b, s, PEER_HEADS, PEER_TOPK * PEER_TOPK)
    top, pos = lax.top_k(cand, PEER_TOPK)
    idx = jnp.take_along_axis(cidx, pos, axis=-1)
    gate = jax.nn.softmax(top, axis=-1)
    nblk = s // PEER_BLOCK

    def blockify(t):
        return jnp.moveaxis(t.reshape(b, nblk, PEER_BLOCK, *t.shape[2:]), 1, 0)

    def one_block(args):
        hb, ib, gb = args
        u = expert_down[ib]
        act = jax.nn.gelu(jnp.einsum('bld,blhkd->blhk', hb, u), approximate=False)
        coef = (gb * act.astype(F32)).astype(hb.dtype)
        vv = expert_up[ib]
        return jnp.einsum('blhk,blhkd->bld', coef, vv)

    out = lax.map(one_block, (blockify(hn), blockify(idx), blockify(gate)))
    return jnp.moveaxis(out, 0, 1).reshape(b, s, d)


def setup_inputs(seed: int = 0) -> dict:
    key = jax.random.key(seed)
    ks = jax.random.split(key, 32)
    L = DEPTH

    def nrm(k, shape, scale):
        return jax.random.normal(k, shape, F32) * scale

    def gain(k, shape):
        return 1.0 + 0.05 * jax.random.normal(k, shape, F32)

    x = nrm(ks[0], (BATCH, SEQ, D_MODEL), 1.0)
    mem = nrm(ks[1], (BATCH, N_MEM, D_MODEL), 1.0)
    offset = jax.random.randint(ks[2], (BATCH, 1), 0, 4096, dtype=jnp.int32)
    positions = jnp.arange(SEQ, dtype=jnp.int32)[None, :] + offset
    norm_mix_w = gain(ks[3], (L, D_MODEL))
    w_in = nrm(ks[4], (L, D_MODEL, IN_PROJ_WIDTH), D_MODEL ** -0.5)
    ssd_conv_w = nrm(ks[5], (L, SSD_CONV, SSD_CONV_DIM), SSD_CONV ** -0.5)
    ssd_conv_b = nrm(ks[6], (L, SSD_CONV_DIM), 0.02)
    dt0 = jnp.exp(jax.random.uniform(ks[7], (L, SSD_HEADS), F32, math.log(1e-3), math.log(1e-1)))
    ssd_dt_bias = dt0 + jnp.log(-jnp.expm1(-dt0))
    ssd_a_log = jnp.log(jax.random.uniform(ks[8], (L, SSD_HEADS), F32, 1.0, 16.0))
    ssd_d = gain(ks[9], (L, SSD_HEADS))
    ssd_norm_w = gain(ks[10], (L, SSD_D_INNER))
    dil_q_norm_w = gain(ks[11], (L, DIL_HEAD_DIM))
    dil_k_norm_w = gain(ks[12], (L, DIL_HEAD_DIM))
    mem_norm_w = gain(ks[13], (L, D_MODEL))
    w_mem_kv = nrm(ks[14], (L, D_MODEL, 2 * MEM_WIDTH), D_MODEL ** -0.5)
    mem_q_norm_w = gain(ks[15], (L, MEM_HEAD_DIM))
    mem_k_norm_w = gain(ks[16], (L, MEM_HEAD_DIM))
    w_ssd_br = nrm(ks[17], (L, SSD_D_INNER, D_MODEL), SSD_D_INNER ** -0.5)
    w_dil_br = nrm(ks[18], (L, DIL_OUT_WIDTH, D_MODEL), DIL_OUT_WIDTH ** -0.5)
    w_mem_br = nrm(ks[19], (L, MEM_WIDTH, D_MODEL), MEM_WIDTH ** -0.5)
    w_out = nrm(ks[20], (L, D_MODEL, D_MODEL), D_MODEL ** -0.5)
    norm_ffn_w = gain(ks[21], (L, D_MODEL))
    peer_w_query = nrm(ks[22], (L, D_MODEL, PEER_HEADS * PEER_QUERY_DIM), D_MODEL ** -0.5)
    peer_sub_keys = nrm(ks[23], (L, PEER_HEADS, 2, PEER_N_KEYS, PEER_QUERY_DIM // 2),
                        (PEER_QUERY_DIM // 2) ** -0.5)
    peer_down = nrm(ks[24], (L, PEER_EXPERTS, D_MODEL), D_MODEL ** -0.5)
    peer_up = nrm(ks[25], (L, PEER_EXPERTS, D_MODEL), PEER_HEADS ** -0.5)
    return {
        'x': x, 'mem': mem, 'positions': positions,
        'norm_mix_w': norm_mix_w, 'w_in': w_in,
        'ssd_conv_w': ssd_conv_w, 'ssd_conv_b': ssd_conv_b, 'ssd_dt_bias': ssd_dt_bias,
        'ssd_a_log': ssd_a_log, 'ssd_d': ssd_d, 'ssd_norm_w': ssd_norm_w,
        'dil_q_norm_w': dil_q_norm_w, 'dil_k_norm_w': dil_k_norm_w,
        'mem_norm_w': mem_norm_w, 'w_mem_kv': w_mem_kv,
        'mem_q_norm_w': mem_q_norm_w, 'mem_k_norm_w': mem_k_norm_w,
        'w_ssd_br': w_ssd_br, 'w_dil_br': w_dil_br, 'w_mem_br': w_mem_br, 'w_out': w_out,
        'norm_ffn_w': norm_ffn_w, 'peer_w_query': peer_w_query, 'peer_sub_keys': peer_sub_keys,
        'peer_down': peer_down, 'peer_up': peer_up,
    }


def reference(x, mem, positions, norm_mix_w, w_in, ssd_conv_w, ssd_conv_b, ssd_dt_bias,
              ssd_a_log, ssd_d, ssd_norm_w, dil_q_norm_w, dil_k_norm_w, mem_norm_w, w_mem_kv,
              mem_q_norm_w, mem_k_norm_w, w_ssd_br, w_dil_br, w_mem_br, w_out, norm_ffn_w,
              peer_w_query, peer_sub_keys, peer_down, peer_up):
    b, s, _ = x.shape
    n_mem = mem.shape[1]
    h = x
    for layer in range(DEPTH):
        u = _rms_norm(h, norm_mix_w[layer])
        proj = u @ w_in[layer]
        z = proj[..., OFF_Z:OFF_XBC]
        xbc = proj[..., OFF_XBC:OFF_DT]
        dt_raw = proj[..., OFF_DT:OFF_DQ]
        dq = proj[..., OFF_DQ:OFF_DK].reshape(b, s, DIL_HEADS, DIL_HEAD_DIM)
        dk = proj[..., OFF_DK:OFF_DV].reshape(b, s, DIL_HEADS, DIL_HEAD_DIM)
        dv = proj[..., OFF_DV:OFF_MQ].reshape(b, s, DIL_HEADS, DIL_HEAD_DIM)
        mq = proj[..., OFF_MQ:OFF_GATE].reshape(b, s, MEM_HEADS, MEM_HEAD_DIM)
        gates = jax.nn.sigmoid(proj[..., OFF_GATE:].astype(F32)).reshape(b, s, N_BRANCHES, D_MODEL)

        xbc = jax.nn.silu(_causal_depthwise_conv(xbc, ssd_conv_w[layer], ssd_conv_b[layer]))
        gn = SSD_GROUPS * SSD_STATE
        xs = xbc[..., :SSD_D_INNER].reshape(b, s, SSD_GROUPS, SSD_HEADS_PER_GROUP, SSD_HEAD_DIM)
        bm = xbc[..., SSD_D_INNER:SSD_D_INNER + gn].reshape(b, s, SSD_GROUPS, SSD_STATE)
        cm = xbc[..., SSD_D_INNER + gn:].reshape(b, s, SSD_GROUPS, SSD_STATE)
        dt = jax.nn.softplus(dt_raw.astype(F32) + ssd_dt_bias[layer].astype(F32))
        dt = dt.reshape(b, s, SSD_GROUPS, SSD_HEADS_PER_GROUP)
        a = -jnp.exp(ssd_a_log[layer].astype(F32)).reshape(SSD_GROUPS, SSD_HEADS_PER_GROUP)
        y = _ssd_chunked(xs, dt, a, bm, cm)
        y = y + ssd_d[layer].astype(F32).reshape(SSD_GROUPS, SSD_HEADS_PER_GROUP)[..., None] * xs.astype(F32)
        y = y.reshape(b, s, SSD_D_INNER) * jax.nn.silu(z.astype(F32))
        yg = y.reshape(b, s, SSD_GROUPS, SSD_D_INNER // SSD_GROUPS)
        yg = yg * lax.rsqrt(jnp.mean(yg * yg, axis=-1, keepdims=True) + EPS)
        y_ssd = (yg.reshape(b, s, SSD_D_INNER) * ssd_norm_w[layer].astype(F32)).astype(h.dtype)

        dq = _partial_rope(_rms_norm(dq, dil_q_norm_w[layer]), positions)
        dk = _partial_rope(_rms_norm(dk, dil_k_norm_w[layer]), positions)
        outs = []
        lses = []
        for gi, (window, dilation) in enumerate(DIL_PAIRS):
            sl = slice(gi * DIL_HEADS_PER_GROUP, (gi + 1) * DIL_HEADS_PER_GROUP)
            o, l = _dilated_group_attention(dq[:, :, sl], dk[:, :, sl], dv[:, :, sl],
                                            dilation, window // dilation)
            outs.append(o.astype(F32))
            lses.append(l)
        wts = jax.nn.softmax(jnp.stack(lses, axis=0), axis=0)
        y_dil = jnp.sum(wts[..., None] * jnp.stack(outs, axis=0), axis=0)
        y_dil = y_dil.reshape(b, s, DIL_OUT_WIDTH).astype(h.dtype)

        mkv = _rms_norm(mem, mem_norm_w[layer]) @ w_mem_kv[layer]
        mk = mkv[..., :MEM_WIDTH].reshape(b, n_mem, MEM_HEADS, MEM_HEAD_DIM)
        mv = mkv[..., MEM_WIDTH:].reshape(b, n_mem, MEM_HEADS, MEM_HEAD_DIM)
        mqn = _rms_norm(mq, mem_q_norm_w[layer])
        mk = _rms_norm(mk, mem_k_norm_w[layer])
        msc = jnp.einsum('bshd,bmhd->bhsm', mqn, mk).astype(F32) * (MEM_HEAD_DIM ** -0.5)
        mp = jax.nn.softmax(msc, axis=-1).astype(mv.dtype)
        y_mem = jnp.einsum('bhsm,bmhd->bshd', mp, mv).reshape(b, s, MEM_WIDTH)

        merged = (gates[:, :, 0] * (y_ssd @ w_ssd_br[layer]).astype(F32)
                  + gates[:, :, 1] * (y_dil @ w_dil_br[layer]).astype(F32)
                  + gates[:, :, 2] * (y_mem @ w_mem_br[layer]).astype(F32))
        h = h + merged.astype(h.dtype) @ w_out[layer]

        hn = _rms_norm(h, norm_ffn_w[layer])
        h = h + _peer(hn, peer_w_query[layer], peer_sub_keys[layer], peer_down[layer], peer_up[layer])
    return h
```

```python
import math
from functools import partial

import jax
import jax.numpy as jnp
from jax import lax
from jax.experimental import pallas as pl
from jax.experimental.pallas import tpu as pltpu

D_MODEL = 1024
N_MEM = 256
EPS = 1e-6
SSD_D_INNER = D_MODEL
SSD_HEAD_DIM = 64
SSD_HEADS = SSD_D_INNER // SSD_HEAD_DIM
SSD_GROUPS = 4
SSD_HEADS_PER_GROUP = SSD_HEADS // SSD_GROUPS
SSD_STATE = 128
SSD_CONV = 4
SSD_CHUNK = 128
SSD_CONV_DIM = SSD_D_INNER + 2 * SSD_GROUPS * SSD_STATE
DIL_PAIRS = ((128, 1), (512, 4), (2048, 16))
DIL_HEADS_PER_GROUP = 4
DIL_HEAD_DIM = 64
DIL_HEADS = len(DIL_PAIRS) * DIL_HEADS_PER_GROUP
DIL_WIDTH = DIL_HEADS * DIL_HEAD_DIM
DIL_OUT_WIDTH = DIL_HEADS_PER_GROUP * DIL_HEAD_DIM
MEM_HEADS = 4
MEM_HEAD_DIM = 192
MEM_WIDTH = MEM_HEADS * MEM_HEAD_DIM
ROPE_THETA = 500000.0
ROPE_FRACTION = 4
PEER_HEADS = 8
PEER_N_KEYS = 128
PEER_QUERY_DIM = 256
PEER_TOPK = 16
PEER_BLOCK = 64
N_BRANCHES = 3
OFF_Z = 0
OFF_XBC = OFF_Z + SSD_D_INNER
OFF_DT = OFF_XBC + SSD_CONV_DIM
OFF_DQ = OFF_DT + SSD_HEADS
OFF_DK = OFF_DQ + DIL_WIDTH
OFF_DV = OFF_DK + DIL_WIDTH
OFF_MQ = OFF_DV + DIL_WIDTH
OFF_GATE = OFF_MQ + MEM_WIDTH
IN_PROJ_WIDTH = OFF_GATE + N_BRANCHES * D_MODEL
F32 = jnp.float32
BF16 = jnp.bfloat16


def _norm_matmul_kernel(x_ref, nw_ref, w_ref, o_ref):
    x = x_ref[...]
    ms = jnp.mean(x * x, axis=-1, keepdims=True)
    u = x * lax.rsqrt(ms + EPS) * nw_ref[...]
    o_ref[...] = jnp.dot(u.astype(BF16), w_ref[...],
                         preferred_element_type=F32).astype(o_ref.dtype)


def _norm_matmul(x2d, norm_w, w_bf16, tm=512, tn=512, out_dtype=F32):
    m, k = x2d.shape
    n = w_bf16.shape[1]
    return pl.pallas_call(
        _norm_matmul_kernel,
        grid=(m // tm, n // tn),
        in_specs=[
            pl.BlockSpec((tm, k), lambda i, j: (i, 0)),
            pl.BlockSpec((1, k), lambda i, j: (0, 0)),
            pl.BlockSpec((k, tn), lambda i, j: (0, j)),
        ],
        out_specs=pl.BlockSpec((tm, tn), lambda i, j: (i, j)),
        out_shape=jax.ShapeDtypeStruct((m, n), out_dtype),
        name="in_proj",
    )(x2d, norm_w.reshape(1, k), w_bf16)


def _rms_norm(t, w):
    tf = t.astype(F32)
    tf = tf * lax.rsqrt(jnp.mean(tf * tf, axis=-1, keepdims=True) + EPS)
    return (tf * w.astype(F32)).astype(t.dtype)


def _partial_rope(t, positions):
    hd = t.shape[-1]
    rot = hd // ROPE_FRACTION
    half = rot // 2
    inv = jnp.exp(-math.log(ROPE_THETA) * (2.0 / rot) * jnp.arange(half, dtype=F32))
    ang = positions.astype(F32)[..., None] * inv
    cos = jnp.cos(ang)[:, :, None, :]
    sin = jnp.sin(ang)[:, :, None, :]
    tf = t.astype(F32)
    t1 = tf[..., :half]
    t2 = tf[..., half:rot]
    out = jnp.concatenate([t1 * cos - t2 * sin, t2 * cos + t1 * sin, tf[..., rot:]], axis=-1)
    return out.astype(t.dtype)


def _causal_depthwise_conv(t, w, bias):
    c = t.shape[-1]
    y = lax.conv_general_dilated(
        t, w[:, None, :].astype(t.dtype), window_strides=(1,),
        padding=[(w.shape[0] - 1, 0)],
        dimension_numbers=('NWC', 'WIO', 'NWC'),
        feature_group_count=c)
    return y + bias.astype(t.dtype)


def _ssd_chunked(xh, dt, a, bmat, cmat):
    b, s, g, r, p = xh.shape
    nc = s // SSD_CHUNK
    x_dt = xh.astype(F32) * dt[..., None]
    da = dt * a

    def chunks(t):
        return jnp.moveaxis(t.reshape(b, nc, SSD_CHUNK, *t.shape[2:]), 1, 0)

    tril = jnp.tril(jnp.ones((SSD_CHUNK, SSD_CHUNK), dtype=bool))
    strict = jnp.tril(jnp.ones((SSD_CHUNK, SSD_CHUNK), dtype=bool), k=-1)

    def step(state, inp):
        xc, dac, bc, cc = inp
        dat = jnp.moveaxis(dac, 1, -1)
        acum = jnp.cumsum(dat, axis=-1)
        seg = jnp.cumsum(jnp.where(strict, dat[..., :, None], 0.0), axis=-2)
        lmat = jnp.exp(jnp.where(tril, seg, -jnp.inf))
        cb = jnp.einsum('bsgn,btgn->bgst', cc, bc)
        y_diag = jnp.einsum('bgst,bgrst,btgrp->bsgrp', cb, lmat, xc)
        y_off = jnp.einsum('bsgn,bgrpn,bgrs->bsgrp', cc, state, jnp.exp(acum))
        decay = jnp.exp(acum[..., -1:] - acum)
        new_state = (state * jnp.exp(acum[..., -1])[..., None, None]
                     + jnp.einsum('btgn,bgrt,btgrp->bgrpn', bc, decay, xc))
        return new_state, y_diag + y_off

    state0 = jnp.zeros((b, g, r, p, bmat.shape[-1]), F32)
    _, ys = lax.scan(step, state0, (chunks(x_dt), chunks(da),
                                    chunks(bmat.astype(F32)), chunks(cmat.astype(F32))))
    return jnp.moveaxis(ys, 0, 1).reshape(b, s, g, r, p)


def _dilated_group_attention(q, k, v, dilation, n_back):
    b, s, h, hd = q.shape
    m = s // dilation
    blk = n_back
    nb = -(-m // blk)
    mp = nb * blk

    def to_sub(t):
        t = t.reshape(b, m, dilation, h, hd)
        t = jnp.pad(t, ((0, 0), (0, mp - m), (0, 0), (0, 0), (0, 0)))
        return t.reshape(b, nb, blk, dilation, h, hd)

    def with_prev(t):
        prev = jnp.pad(t[:, :-1], ((0, 0), (1, 0), (0, 0), (0, 0), (0, 0), (0, 0)))
        return jnp.concatenate([prev, t], axis=2)

    qs = to_sub(q)
    kk = with_prev(to_sub(k))
    vv = with_prev(to_sub(v))
    scores = jnp.einsum('bnqrhd,bnkrhd->bnrhqk', qs, kk).astype(F32) * (hd ** -0.5)
    qi = jnp.arange(blk)[:, None]
    ki = jnp.arange(2 * blk)[None, :]
    dist = qi + blk - ki
    band = (dist >= 0) & (dist <= n_back)
    valid = band[None] & ((jnp.arange(nb)[:, None, None] > 0) | (ki >= blk)[None])
    scores = jnp.where(valid[None, :, None, None], scores, -jnp.inf)
    lse = jax.nn.logsumexp(scores, axis=-1)
    probs = jnp.exp(scores - lse[..., None]).astype(v.dtype)
    out = jnp.einsum('bnrhqk,bnkrhd->bnqrhd', probs, vv)
    out = out.reshape(b, mp, dilation, h, hd)[:, :m].reshape(b, s, h, hd)
    lse = lse.transpose(0, 1, 4, 2, 3).reshape(b, mp, dilation, h)[:, :m].reshape(b, s, h)
    return out, lse


def _peer(hn, w_query, sub_keys, expert_down, expert_up):
    b, s, d = hn.shape
    q = (hn @ w_query).reshape(b, s, PEER_HEADS, 2, PEER_QUERY_DIM // 2)
    sc = jnp.einsum('bshcd,hckd->bshck', q, sub_keys).astype(F32)
    v1, i1 = lax.top_k(sc[..., 0, :], PEER_TOPK)
    v2, i2 = lax.top_k(sc[..., 1, :], PEER_TOPK)
    cand = (v1[..., :, None] + v2[..., None, :]).reshape(b, s, PEER_HEADS, PEER_TOPK * PEER_TOPK)
    cidx = (i1[..., :, None] * PEER_N_KEYS + i2[..., None, :]).reshape(b, s, PEER_HEADS, PEER_TOPK * PEER_TOPK)
    top, pos = lax.top_k(cand, PEER_TOPK)
    idx = jnp.take_along_axis(cidx, pos, axis=-1)
    gate = jax.nn.softmax(top, axis=-1)
    nblk = s // PEER_BLOCK

    def blockify(t):
        return jnp.moveaxis(t.reshape(b, nblk, PEER_BLOCK, *t.shape[2:]), 1, 0)

    def one_block(args):
        hb, ib, gb = args
        u = expert_down[ib]
        act = jax.nn.gelu(jnp.einsum('bld,blhkd->blhk', hb, u), approximate=False)
        coef = (gb * act.astype(F32)).astype(hb.dtype)
        vv = expert_up[ib]
        return jnp.einsum('blhk,blhkd->bld', coef, vv)

    out = lax.map(one_block, (blockify(hn), blockify(idx), blockify(gate)))
    return jnp.moveaxis(out, 0, 1).reshape(b, s, d)


def kernel(x, mem, positions, norm_mix_w, w_in, ssd_conv_w, ssd_conv_b, ssd_dt_bias,
           ssd_a_log, ssd_d, ssd_norm_w, dil_q_norm_w, dil_k_norm_w, mem_norm_w, w_mem_kv,
           mem_q_norm_w, mem_k_norm_w, w_ssd_br, w_dil_br, w_mem_br, w_out, norm_ffn_w,
           peer_w_query, peer_sub_keys, peer_down, peer_up):
    b, s, _ = x.shape
    n_mem = mem.shape[1]
    h = x
    layer = 0
    n_pad = 9728
    w_pad = jnp.pad(w_in[layer], ((0, 0), (0, n_pad - IN_PROJ_WIDTH))).astype(BF16)
    proj = _norm_matmul(h.reshape(b * s, D_MODEL), norm_mix_w[layer], w_pad)
    proj = proj[:, :IN_PROJ_WIDTH].reshape(b, s, IN_PROJ_WIDTH)
    z = proj[..., OFF_Z:OFF_XBC]
    xbc = proj[..., OFF_XBC:OFF_DT]
    dt_raw = proj[..., OFF_DT:OFF_DQ]
    dq = proj[..., OFF_DQ:OFF_DK].reshape(b, s, DIL_HEADS, DIL_HEAD_DIM)
    dk = proj[..., OFF_DK:OFF_DV].reshape(b, s, DIL_HEADS, DIL_HEAD_DIM)
    dv = proj[..., OFF_DV:OFF_MQ].reshape(b, s, DIL_HEADS, DIL_HEAD_DIM)
    mq = proj[..., OFF_MQ:OFF_GATE].reshape(b, s, MEM_HEADS, MEM_HEAD_DIM)
    gates = jax.nn.sigmoid(proj[..., OFF_GATE:].astype(F32)).reshape(b, s, N_BRANCHES, D_MODEL)

    xbc = jax.nn.silu(_causal_depthwise_conv(xbc, ssd_conv_w[layer], ssd_conv_b[layer]))
    gn = SSD_GROUPS * SSD_STATE
    xs = xbc[..., :SSD_D_INNER].reshape(b, s, SSD_GROUPS, SSD_HEADS_PER_GROUP, SSD_HEAD_DIM)
    bm = xbc[..., SSD_D_INNER:SSD_D_INNER + gn].reshape(b, s, SSD_GROUPS, SSD_STATE)
    cm = xbc[..., SSD_D_INNER + gn:].reshape(b, s, SSD_GROUPS, SSD_STATE)
    dt = jax.nn.softplus(dt_raw.astype(F32) + ssd_dt_bias[layer].astype(F32))
    dt = dt.reshape(b, s, SSD_GROUPS, SSD_HEADS_PER_GROUP)
    a = -jnp.exp(ssd_a_log[layer].astype(F32)).reshape(SSD_GROUPS, SSD_HEADS_PER_GROUP)
    y = _ssd_chunked(xs, dt, a, bm, cm)
    y = y + ssd_d[layer].astype(F32).reshape(SSD_GROUPS, SSD_HEADS_PER_GROUP)[..., None] * xs.astype(F32)
    y = y.reshape(b, s, SSD_D_INNER) * jax.nn.silu(z.astype(F32))
    yg = y.reshape(b, s, SSD_GROUPS, SSD_D_INNER // SSD_GROUPS)
    yg = yg * lax.rsqrt(jnp.mean(yg * yg, axis=-1, keepdims=True) + EPS)
    y_ssd = (yg.reshape(b, s, SSD_D_INNER) * ssd_norm_w[layer].astype(F32)).astype(h.dtype)

    dq = _partial_rope(_rms_norm(dq, dil_q_norm_w[layer]), positions)
    dk = _partial_rope(_rms_norm(dk, dil_k_norm_w[layer]), positions)
    outs = []
    lses = []
    for gi, (window, dilation) in enumerate(DIL_PAIRS):
        sl = slice(gi * DIL_HEADS_PER_GROUP, (gi + 1) * DIL_HEADS_PER_GROUP)
        o, l = _dilated_group_attention(dq[:, :, sl], dk[:, :, sl], dv[:, :, sl],
                                        dilation, window // dilation)
        outs.append(o.astype(F32))
        lses.append(l)
    wts = jax.nn.softmax(jnp.stack(lses, axis=0), axis=0)
    y_dil = jnp.sum(wts[..., None] * jnp.stack(outs, axis=0), axis=0)
    y_dil = y_dil.reshape(b, s, DIL_OUT_WIDTH).astype(h.dtype)

    mkv = _rms_norm(mem, mem_norm_w[layer]) @ w_mem_kv[layer]
    mk = mkv[..., :MEM_WIDTH].reshape(b, n_mem, MEM_HEADS, MEM_HEAD_DIM)
    mv = mkv[..., MEM_WIDTH:].reshape(b, n_mem, MEM_HEADS, MEM_HEAD_DIM)
    mqn = _rms_norm(mq, mem_q_norm_w[layer])
    mk = _rms_norm(mk, mem_k_norm_w[layer])
    msc = jnp.einsum('bshd,bmhd->bhsm', mqn, mk).astype(F32) * (MEM_HEAD_DIM ** -0.5)
    mp = jax.nn.softmax(msc, axis=-1).astype(mv.dtype)
    y_mem = jnp.einsum('bhsm,bmhd->bshd', mp, mv).reshape(b, s, MEM_WIDTH)

    merged = (gates[:, :, 0] * (y_ssd @ w_ssd_br[layer]).astype(F32)
              + gates[:, :, 1] * (y_dil @ w_dil_br[layer]).astype(F32)
              + gates[:, :, 2] * (y_mem @ w_mem_br[layer]).astype(F32))
    h = h + merged.astype(h.dtype) @ w_out[layer]

    hn = _rms_norm(h, norm_ffn_w[layer])
    h = h + _peer(hn, peer_w_query[layer], peer_sub_keys[layer], peer_down[layer], peer_up[layer])
    return h
```

```python
import math
from functools import partial

import jax
import jax.numpy as jnp
from jax import lax
from jax.experimental import pallas as pl
from jax.experimental.pallas import tpu as pltpu

D_MODEL = 1024
N_MEM = 256
EPS = 1e-6
SSD_D_INNER = D_MODEL
SSD_HEAD_DIM = 64
SSD_HEADS = SSD_D_INNER // SSD_HEAD_DIM
SSD_GROUPS = 4
SSD_HEADS_PER_GROUP = SSD_HEADS // SSD_GROUPS
SSD_STATE = 128
SSD_CONV = 4
SSD_CHUNK = 128
SSD_CONV_DIM = SSD_D_INNER + 2 * SSD_GROUPS * SSD_STATE
DIL_PAIRS = ((128, 1), (512, 4), (2048, 16))
DIL_HEADS_PER_GROUP = 4
DIL_HEAD_DIM = 64
DIL_HEADS = len(DIL_PAIRS) * DIL_HEADS_PER_GROUP
DIL_WIDTH = DIL_HEADS * DIL_HEAD_DIM
DIL_OUT_WIDTH = DIL_HEADS_PER_GROUP * DIL_HEAD_DIM
MEM_HEADS = 4
MEM_HEAD_DIM = 192
MEM_WIDTH = MEM_HEADS * MEM_HEAD_DIM
ROPE_THETA = 500000.0
ROPE_FRACTION = 4
PEER_HEADS = 8
PEER_N_KEYS = 128
PEER_QUERY_DIM = 256
PEER_TOPK = 16
PEER_BLOCK = 64
N_BRANCHES = 3
OFF_Z = 0
OFF_XBC = OFF_Z + SSD_D_INNER
OFF_DT = OFF_XBC + SSD_CONV_DIM
OFF_DQ = OFF_DT + SSD_HEADS
OFF_DK = OFF_DQ + DIL_WIDTH
OFF_DV = OFF_DK + DIL_WIDTH
OFF_MQ = OFF_DV + DIL_WIDTH
OFF_GATE = OFF_MQ + MEM_WIDTH
IN_PROJ_WIDTH = OFF_GATE + N_BRANCHES * D_MODEL
F32 = jnp.float32
BF16 = jnp.bfloat16


def _norm_matmul_kernel(x_ref, nw_ref, w_ref, o_ref):
    x = x_ref[...]
    ms = jnp.mean(x * x, axis=-1, keepdims=True)
    u = x * lax.rsqrt(ms + EPS) * nw_ref[...]
    o_ref[...] = jnp.dot(u.astype(BF16), w_ref[...],
                         preferred_element_type=F32).astype(o_ref.dtype)


def _norm_matmul(x2d, norm_w, w_bf16, tm=512, tn=512, out_dtype=F32):
    m, k = x2d.shape
    n = w_bf16.shape[1]
    return pl.pallas_call(
        _norm_matmul_kernel,
        grid=(m // tm, n // tn),
        in_specs=[
            pl.BlockSpec((tm, k), lambda i, j: (i, 0)),
            pl.BlockSpec((1, k), lambda i, j: (0, 0)),
            pl.BlockSpec((k, tn), lambda i, j: (0, j)),
        ],
        out_specs=pl.BlockSpec((tm, tn), lambda i, j: (i, j)),
        out_shape=jax.ShapeDtypeStruct((m, n), out_dtype),
        name="in_proj",
    )(x2d, norm_w.reshape(1, k), w_bf16)


def _rms_norm(t, w):
    tf = t.astype(F32)
    tf = tf * lax.rsqrt(jnp.mean(tf * tf, axis=-1, keepdims=True) + EPS)
    return (tf * w.astype(F32)).astype(t.dtype)


def _partial_rope(t, positions):
    hd = t.shape[-1]
    rot = hd // ROPE_FRACTION
    half = rot // 2
    inv = jnp.exp(-math.log(ROPE_THETA) * (2.0 / rot) * jnp.arange(half, dtype=F32))
    ang = positions.astype(F32)[..., None] * inv
    cos = jnp.cos(ang)[:, :, None, :]
    sin = jnp.sin(ang)[:, :, None, :]
    tf = t.astype(F32)
    t1 = tf[..., :half]
    t2 = tf[..., half:rot]
    out = jnp.concatenate([t1 * cos - t2 * sin, t2 * cos + t1 * sin, tf[..., rot:]], axis=-1)
    return out.astype(t.dtype)


def _causal_depthwise_conv(t, w, bias):
    c = t.shape[-1]
    y = lax.conv_general_dilated(
        t, w[:, None, :].astype(t.dtype), window_strides=(1,),
        padding=[(w.shape[0] - 1, 0)],
        dimension_numbers=('NWC', 'WIO', 'NWC'),
        feature_group_count=c)
    return y + bias.astype(t.dtype)


def _ssd_chunked(xh, dt, a, bmat, cmat):
    b, s, g, r, p = xh.shape
    nc = s // SSD_CHUNK
    x_dt = xh.astype(F32) * dt[..., None]
    da = dt * a

    def chunks(t):
        return jnp.moveaxis(t.reshape(b, nc, SSD_CHUNK, *t.shape[2:]), 1, 0)

    tril = jnp.tril(jnp.ones((SSD_CHUNK, SSD_CHUNK), dtype=bool))
    strict = jnp.tril(jnp.ones((SSD_CHUNK, SSD_CHUNK), dtype=bool), k=-1)

    def step(state, inp):
        xc, dac, bc, cc = inp
        dat = jnp.moveaxis(dac, 1, -1)
        acum = jnp.cumsum(dat, axis=-1)
        seg = jnp.cumsum(jnp.where(strict, dat[..., :, None], 0.0), axis=-2)
        lmat = jnp.exp(jnp.where(tril, seg, -jnp.inf))
        cb = jnp.einsum('bsgn,btgn->bgst', cc, bc)
        y_diag = jnp.einsum('bgst,bgrst,btgrp->bsgrp', cb, lmat, xc)
        y_off = jnp.einsum('bsgn,bgrpn,bgrs->bsgrp', cc, state, jnp.exp(acum))
        decay = jnp.exp(acum[..., -1:] - acum)
        new_state = (state * jnp.exp(acum[..., -1])[..., None, None]
                     + jnp.einsum('btgn,bgrt,btgrp->bgrpn', bc, decay, xc))
        return new_state, y_diag + y_off

    state0 = jnp.zeros((b, g, r, p, bmat.shape[-1]), F32)
    _, ys = lax.scan(step, state0, (chunks(x_dt), chunks(da),
                                    chunks(bmat.astype(F32)), chunks(cmat.astype(F32))))
    return jnp.moveaxis(ys, 0, 1).reshape(b, s, g, r, p)


def _dilated_group_attention(q, k, v, dilation, n_back):
    b, s, h, hd = q.shape
    m = s // dilation
    blk = n_back
    nb = -(-m // blk)
    mp = nb * blk

    def to_sub(t):
        t = t.reshape(b, m, dilation, h, hd)
        t = jnp.pad(t, ((0, 0), (0, mp - m), (0, 0), (0, 0), (0, 0)))
        return t.reshape(b, nb, blk, dilation, h, hd)

    def with_prev(t):
        prev = jnp.pad(t[:, :-1], ((0, 0), (1, 0), (0, 0), (0, 0), (0, 0), (0, 0)))
        return jnp.concatenate([prev, t], axis=2)

    qs = to_sub(q)
    kk = with_prev(to_sub(k))
    vv = with_prev(to_sub(v))
    scores = jnp.einsum('bnqrhd,bnkrhd->bnrhqk', qs, kk).astype(F32) * (hd ** -0.5)
    qi = jnp.arange(blk)[:, None]
    ki = jnp.arange(2 * blk)[None, :]
    dist = qi + blk - ki
    band = (dist >= 0) & (dist <= n_back)
    valid = band[None] & ((jnp.arange(nb)[:, None, None] > 0) | (ki >= blk)[None])
    scores = jnp.where(valid[None, :, None, None], scores, -jnp.inf)
    lse = jax.nn.logsumexp(scores, axis=-1)
    probs = jnp.exp(scores - lse[..., None]).astype(v.dtype)
    out = jnp.einsum('bnrhqk,bnkrhd->bnqrhd', probs, vv)
    out = out.reshape(b, mp, dilation, h, hd)[:, :m].reshape(b, s, h, hd)
    lse = lse.transpose(0, 1, 4, 2, 3).reshape(b, mp, dilation, h)[:, :m].reshape(b, s, h)
    return out, lse


PEER_TOKENS = 512
PEER_A_PER_STEP = 4
VMEM_LIMIT_BYTES = 56 * 1024 * 1024


def _oddeven_merge(lo, hi, r):
    step = r * 2
    if step < hi - lo:
        yield from _oddeven_merge(lo, hi, step)
        yield from _oddeven_merge(lo + r, hi, step)
        yield from [(i, i + r) for i in range(lo + r, hi - r, step)]
    else:
        yield (lo, lo + r)


def _oddeven_merge_sort(lo, hi):
    if hi - lo >= 1:
        mid = lo + (hi - lo) // 2
        yield from _oddeven_merge_sort(lo, mid)
        yield from _oddeven_merge_sort(mid + 1, hi)
        yield from _oddeven_merge(lo, hi, 1)


_SORT16 = tuple(_oddeven_merge_sort(0, PEER_TOPK - 1))


def _cmpx(a, i, j):
    hi = jnp.maximum(a[i], a[j])
    lo = jnp.minimum(a[i], a[j])
    a[i], a[j] = hi, lo


def _sort16_desc(a):
    for i, j in _SORT16:
        _cmpx(a, i, j)
    return a


def _bitonic16_desc(a):
    for d in (8, 4, 2, 1):
        for i in range(PEER_TOPK):
            if not i & d:
                _cmpx(a, i, i + d)
    return a


def _merge_top16(a, b):
    c = [jnp.maximum(a[k], b[PEER_TOPK - 1 - k]) for k in range(PEER_TOPK)]
    return _bitonic16_desc(c)


def _merge_sublanes(a):
    for sh in (4, 2, 1):
        b = [pltpu.roll(x, sh, axis=0) for x in a]
        a = _merge_top16(a, b)
    return a


def _top16_rows(s):
    rows = [s[8 * i:8 * i + 8, :] for i in range(PEER_N_KEYS // 8)]
    return _merge_sublanes(_sort16_desc(rows))


def _peer_route_kernel(h_ref, nw_ref, wqt_ref, keys_ref,
                       hnt_ref, theta_ref, f1_ref, s2_ref, e2_ref):
    t = h_ref.shape[0]
    x = h_ref[...]
    ms = jnp.mean(x * x, axis=-1, keepdims=True)
    hn = x * lax.rsqrt(ms + EPS) * nw_ref[...]
    hnt = hn.T.astype(BF16)
    hnt_ref[...] = hnt
    sub = lax.broadcasted_iota(jnp.int32, (8, t), 0)

    def distribute(vs):
        out = vs[7]
        for r in range(6, -1, -1):
            out = jnp.where(sub == r, vs[r], out)
        return out

    for h in range(PEER_HEADS):
        qt = jnp.dot(wqt_ref[h * PEER_QUERY_DIM:(h + 1) * PEER_QUERY_DIM, :], hnt,
                     preferred_element_type=F32).astype(BF16)
        half = PEER_QUERY_DIM // 2
        s1 = jnp.dot(keys_ref[h, 0], qt[:half], preferred_element_type=F32)
        s2 = jnp.dot(keys_ref[h, 1], qt[half:], preferred_element_type=F32)
        v1 = _top16_rows(s1)
        v2 = _top16_rows(s2)
        a_lo = distribute(v1[:8])
        a_hi = distribute(v1[8:])
        lo = [a_lo + v2[j] for j in range(PEER_TOPK)]
        hi = [a_hi + v2[j] for j in range(PEER_TOPK)]
        tau = _merge_sublanes(_merge_top16(lo, hi))[PEER_TOPK - 1]
        e_lo = jnp.exp(a_lo - v1[0])
        e_hi = jnp.exp(a_hi - v1[0])
        zp = jnp.zeros((8, t), F32)
        for j in range(PEER_TOPK):
            e2j = jnp.exp(v2[j] - v2[0])
            zp = zp + jnp.where(lo[j] >= tau, e_lo * e2j, 0.0)
            zp = zp + jnp.where(hi[j] >= tau, e_hi * e2j, 0.0)
        z = jnp.sum(zp, axis=0, keepdims=True)
        tau_r = tau[0:1, :]
        theta = jnp.full(s1.shape, jnp.inf, F32)
        for j in range(PEER_TOPK):
            v2j = v2[j][0:1, :]
            theta = jnp.where(s1 + v2j >= tau_r, v2j, theta)
        theta_ref[h] = theta
        f1_ref[h] = jnp.exp(s1 - v1[0][0:1, :]) / z
        s2_ref[h] = s2
        e2_ref[h] = jnp.exp(s2 - v2[0][0:1, :])


def _peer_route(h2d, norm_w, wqt, keys):
    n, d = h2d.shape
    t = PEER_TOKENS
    route = jax.ShapeDtypeStruct((PEER_HEADS, PEER_N_KEYS, n), F32)
    route_spec = pl.BlockSpec((PEER_HEADS, PEER_N_KEYS, t), lambda i: (0, 0, i))
    return pl.pallas_call(
        _peer_route_kernel,
        grid=(n // t,),
        in_specs=[
            pl.BlockSpec((t, d), lambda i: (i, 0)),
            pl.BlockSpec((1, d), lambda i: (0, 0)),
            pl.BlockSpec(wqt.shape, lambda i: (0, 0)),
            pl.BlockSpec(keys.shape, lambda i: (0, 0, 0, 0)),
        ],
        out_specs=[pl.BlockSpec((d, t), lambda i: (0, i)),
                   route_spec, route_spec, route_spec, route_spec],
        out_shape=[jax.ShapeDtypeStruct((d, n), BF16), route, route, route, route],
        compiler_params=pltpu.CompilerParams(vmem_limit_bytes=VMEM_LIMIT_BYTES),
        name="peer_route",
    )(h2d, norm_w.reshape(1, d), wqt, keys)


def _peer_expert_kernel(h_ref, hnt_ref, theta_ref, f1_ref, s2_ref, e2_ref,
                        down_ref, upt_ref, o_ref, acc_ref):
    step = pl.program_id(1)

    @pl.when(step == 0)
    def _():
        acc_ref[...] = jnp.zeros_like(acc_ref)

    s = jnp.dot(down_ref[...], hnt_ref[...], preferred_element_type=F32)
    act = 0.5 * s * (1.0 + lax.erf(s * (2.0 ** -0.5)))
    coefs = []
    for q in range(PEER_A_PER_STEP):
        a = step * PEER_A_PER_STEP + q
        w = None
        for h in range(PEER_HEADS):
            th = theta_ref[h, pl.ds(a, 1), :]
            f1 = f1_ref[h, pl.ds(a, 1), :]
            wh = jnp.where(s2_ref[h] >= th, e2_ref[h], 0.0) * f1
            w = wh if w is None else w + wh
        coefs.append((w * act[q * PEER_N_KEYS:(q + 1) * PEER_N_KEYS, :]).astype(BF16))
    coef = jnp.concatenate(coefs, axis=0)
    acc_ref[...] += jnp.dot(upt_ref[...], coef, preferred_element_type=F32)

    @pl.when(step == pl.num_programs(1) - 1)
    def _():
        o_ref[...] = h_ref[...] + acc_ref[...].T


def _peer_expert(h2d, hnt, theta, f1, s2, e2, down_bf, upt_bf):
    n, d = h2d.shape
    t = PEER_TOKENS
    eb = PEER_A_PER_STEP * PEER_N_KEYS
    route_spec = pl.BlockSpec((PEER_HEADS, PEER_N_KEYS, t), lambda i, j: (0, 0, i))
    return pl.pallas_call(
        _peer_expert_kernel,
        grid=(n // t, PEER_N_KEYS // PEER_A_PER_STEP),
        in_specs=[
            pl.BlockSpec((t, d), lambda i, j: (i, 0)),
            pl.BlockSpec((d, t), lambda i, j: (0, i)),
            route_spec, route_spec, route_spec, route_spec,
            pl.BlockSpec((eb, d), lambda i, j: (j, 0)),
            pl.BlockSpec((d, eb), lambda i, j: (0, j)),
        ],
        out_specs=pl.BlockSpec((t, d), lambda i, j: (i, 0)),
        out_shape=jax.ShapeDtypeStruct((n, d), F32),
        scratch_shapes=[pltpu.VMEM((d, t), F32)],
        compiler_params=pltpu.CompilerParams(
            dimension_semantics=("arbitrary", "arbitrary"),
            vmem_limit_bytes=VMEM_LIMIT_BYTES),
        name="peer_expert",
    )(h2d, hnt, theta, f1, s2, e2, down_bf, upt_bf)


def _peer_block(h2d, norm_w, w_query, sub_keys, expert_down, expert_up):
    wqt = w_query.T.astype(BF16)
    hnt, theta, f1, s2, e2 = _peer_route(h2d, norm_w, wqt, sub_keys.astype(BF16))
    return _peer_expert(h2d, hnt, theta, f1, s2, e2,
                        expert_down.astype(BF16), expert_up.T.astype(BF16))


def kernel(x, mem, positions, norm_mix_w, w_in, ssd_conv_w, ssd_conv_b, ssd_dt_bias,
           ssd_a_log, ssd_d, ssd_norm_w, dil_q_norm_w, dil_k_norm_w, mem_norm_w, w_mem_kv,
           mem_q_norm_w, mem_k_norm_w, w_ssd_br, w_dil_br, w_mem_br, w_out, norm_ffn_w,
           peer_w_query, peer_sub_keys, peer_down, peer_up):
    b, s, _ = x.shape
    n_mem = mem.shape[1]
    h = x
    layer = 0
    n_pad = 9728
    w_pad = jnp.pad(w_in[layer], ((0, 0), (0, n_pad - IN_PROJ_WIDTH))).astype(BF16)
    proj = _norm_matmul(h.reshape(b * s, D_MODEL), norm_mix_w[layer], w_pad)
    proj = proj[:, :IN_PROJ_WIDTH].reshape(b, s, IN_PROJ_WIDTH)
    z = proj[..., OFF_Z:OFF_XBC]
    xbc = proj[..., OFF_XBC:OFF_DT]
    dt_raw = proj[..., OFF_DT:OFF_DQ]
    dq = proj[..., OFF_DQ:OFF_DK].reshape(b, s, DIL_HEADS, DIL_HEAD_DIM)
    dk = proj[..., OFF_DK:OFF_DV].reshape(b, s, DIL_HEADS, DIL_HEAD_DIM)
    dv = proj[..., OFF_DV:OFF_MQ].reshape(b, s, DIL_HEADS, DIL_HEAD_DIM)
    mq = proj[..., OFF_MQ:OFF_GATE].reshape(b, s, MEM_HEADS, MEM_HEAD_DIM)
    gates = jax.nn.sigmoid(proj[..., OFF_GATE:].astype(F32)).reshape(b, s, N_BRANCHES, D_MODEL)

    xbc = jax.nn.silu(_causal_depthwise_conv(xbc, ssd_conv_w[layer], ssd_conv_b[layer]))
    gn = SSD_GROUPS * SSD_STATE
    xs = xbc[..., :SSD_D_INNER].reshape(b, s, SSD_GROUPS, SSD_HEADS_PER_GROUP, SSD_HEAD_DIM)
    bm = xbc[..., SSD_D_INNER:SSD_D_INNER + gn].reshape(b, s, SSD_GROUPS, SSD_STATE)
    cm = xbc[..., SSD_D_INNER + gn:].reshape(b, s, SSD_GROUPS, SSD_STATE)
    dt = jax.nn.softplus(dt_raw.astype(F32) + ssd_dt_bias[layer].astype(F32))
    dt = dt.reshape(b, s, SSD_GROUPS, SSD_HEADS_PER_GROUP)
    a = -jnp.exp(ssd_a_log[layer].astype(F32)).reshape(SSD_GROUPS, SSD_HEADS_PER_GROUP)
    y = _ssd_chunked(xs, dt, a, bm, cm)
    y = y + ssd_d[layer].astype(F32).reshape(SSD_GROUPS, SSD_HEADS_PER_GROUP)[..., None] * xs.astype(F32)
    y = y.reshape(b, s, SSD_D_INNER) * jax.nn.silu(z.astype(F32))
    yg = y.reshape(b, s, SSD_GROUPS, SSD_D_INNER // SSD_GROUPS)
    yg = yg * lax.rsqrt(jnp.mean(yg * yg, axis=-1, keepdims=True) + EPS)
    y_ssd = (yg.reshape(b, s, SSD_D_INNER) * ssd_norm_w[layer].astype(F32)).astype(h.dtype)

    dq = _partial_rope(_rms_norm(dq, dil_q_norm_w[layer]), positions)
    dk = _partial_rope(_rms_norm(dk, dil_k_norm_w[layer]), positions)
    outs = []
    lses = []
    for gi, (window, dilation) in enumerate(DIL_PAIRS):
        sl = slice(gi * DIL_HEADS_PER_GROUP, (gi + 1) * DIL_HEADS_PER_GROUP)
        o, l = _dilated_group_attention(dq[:, :, sl], dk[:, :, sl], dv[:, :, sl],
                                        dilation, window // dilation)
        outs.append(o.astype(F32))
        lses.append(l)
    wts = jax.nn.softmax(jnp.stack(lses, axis=0), axis=0)
    y_dil = jnp.sum(wts[..., None] * jnp.stack(outs, axis=0), axis=0)
    y_dil = y_dil.reshape(b, s, DIL_OUT_WIDTH).astype(h.dtype)

    mkv = _rms_norm(mem, mem_norm_w[layer]) @ w_mem_kv[layer]
    mk = mkv[..., :MEM_WIDTH].reshape(b, n_mem, MEM_HEADS, MEM_HEAD_DIM)
    mv = mkv[..., MEM_WIDTH:].reshape(b, n_mem, MEM_HEADS, MEM_HEAD_DIM)
    mqn = _rms_norm(mq, mem_q_norm_w[layer])
    mk = _rms_norm(mk, mem_k_norm_w[layer])
    msc = jnp.einsum('bshd,bmhd->bhsm', mqn, mk).astype(F32) * (MEM_HEAD_DIM ** -0.5)
    mp = jax.nn.softmax(msc, axis=-1).astype(mv.dtype)
    y_mem = jnp.einsum('bhsm,bmhd->bshd', mp, mv).reshape(b, s, MEM_WIDTH)

    merged = (gates[:, :, 0] * (y_ssd @ w_ssd_br[layer]).astype(F32)
              + gates[:, :, 1] * (y_dil @ w_dil_br[layer]).astype(F32)
              + gates[:, :, 2] * (y_mem @ w_mem_br[layer]).astype(F32))
    h = h + merged.astype(h.dtype) @ w_out[layer]

    h = _peer_block(h.reshape(b * s, D_MODEL), norm_ffn_w[layer], peer_w_query[layer],
                    peer_sub_keys[layer], peer_down[layer], peer_up[layer])
    return h.reshape(b, s, D_MODEL)
```

```python
import math

import jax
import jax.numpy as jnp
from jax import lax
from jax.experimental import pallas as pl
from jax.experimental.pallas import tpu as pltpu

D_MODEL = 1024
EPS = 1e-6
SSD_HEAD_DIM = 64
SSD_HEADS = 16
SSD_GROUPS = 4
SSD_HEADS_PER_GROUP = 4
SSD_STATE = 128
SSD_CONV = 4
SSD_CHUNK = 128
SSD_CONV_DIM = 2048
DIL_PAIRS = ((128, 1), (512, 4), (2048, 16))
DIL_HEAD_DIM = 64
DIL_WIDTH = 768
DIL_OUT_WIDTH = 256
MEM_HEADS = 4
MEM_HEAD_DIM = 192
MEM_HEAD_PAD = 256
MEM_WIDTH = 768
ROPE_THETA = 500000.0
ROPE_DIMS = 16
PEER_HEADS = 8
PEER_N_KEYS = 128
PEER_QUERY_DIM = 256
PEER_TOPK = 16
OFF_Z = 0
OFF_XBC = 1024
OFF_DT = 3072
OFF_DQ = 3088
OFF_DK = OFF_DQ + DIL_WIDTH
OFF_DV = OFF_DK + DIL_WIDTH
OFF_MQ = OFF_DV + DIL_WIDTH
OFF_GATE = OFF_MQ + MEM_WIDTH
F32 = jnp.float32
BF16 = jnp.bfloat16
HIGHEST = lax.Precision.HIGHEST
LANES = 128
VMEM_LIMIT_BYTES = 56 * 1024 * 1024

P_Z = 0
P_XBC = 1024
P_DT = 3072
P_Q = 3200
P_K = P_Q + DIL_WIDTH
P_V = P_K + DIL_WIDTH
P_MQ = P_V + DIL_WIDTH
P_GATE = P_MQ + MEM_HEADS * MEM_HEAD_PAD
P_WIDTH = P_GATE + 3 * D_MODEL


def _resident(shape):
    nd = len(shape)
    return pl.BlockSpec(shape, lambda *_: (0,) * nd, pipeline_mode=pl.Buffered(1))


def _sigmoid(x):
    return 1.0 / (1.0 + jnp.exp(-x))


def _dot(a, b):
    return jnp.dot(a, b, preferred_element_type=F32)


def _dot_nt(a, b):
    return lax.dot_general(a, b, (((1,), (1,)), ((), ())), preferred_element_type=F32)


def _dot_f32(a, b):
    return jnp.dot(a, b, precision=HIGHEST, preferred_element_type=F32)


IN_PROJ_ROWS = 512


def _in_proj_kernel(x_ref, pos_ref, nw_ref, w_ref, qnw_ref, knw_ref, mqnw_ref, invp_ref,
                    z_ref, xbc_ref, dt_ref, q_ref, k_ref, v_ref, mq_ref, gate_ref, u_ref):
    x = x_ref[...]
    ms = jnp.mean(x * x, axis=-1, keepdims=True)
    u_ref[...] = (x * lax.rsqrt(ms + EPS) * nw_ref[...]).astype(BF16)

    def seg(off, width):
        return _dot(u_ref[...], w_ref[:, off:off + width])

    for j in range(0, 1024, 256):
        z_ref[:, j:j + 256] = seg(P_Z + j, 256).astype(BF16)
    for j in range(0, SSD_CONV_DIM, 256):
        xbc_ref[:, j:j + 256] = seg(P_XBC + j, 256).astype(BF16)
    dt_ref[...] = seg(P_DT, LANES)
    for j in range(0, DIL_WIDTH, 256):
        v_ref[:, j:j + 256] = seg(P_V + j, 256).astype(BF16)
    for j in range(0, 3 * D_MODEL, 256):
        gate_ref[:, j:j + 256] = seg(P_GATE + j, 256).astype(BF16)

    rows = x.shape[0]
    ang = pos_ref[...] * invp_ref[...]
    cos = jnp.cos(ang)
    sin = jnp.sin(ang)
    lane = lax.broadcasted_iota(jnp.int32, (rows, LANES), 1)
    l64 = lane & (DIL_HEAD_DIM - 1)
    first = l64 < ROPE_DIMS // 2
    rotated = l64 < ROPE_DIMS
    cm = jnp.where(rotated, cos, 1.0)
    sm = jnp.where(rotated, jnp.where(first, -sin, sin), 0.0)
    left = lane < DIL_HEAD_DIM

    def qk_norm_rope(off, w_row_ref, o_ref):
        for j in range(0, DIL_WIDTH, 256):
            t2 = seg(off + j, 256)
            for i in range(2):
                t = t2[:, i * LANES:(i + 1) * LANES]
                sq = t * t
                ssl = jnp.sum(jnp.where(left, sq, 0.0), axis=-1, keepdims=True)
                ssr = jnp.sum(jnp.where(left, 0.0, sq), axis=-1, keepdims=True)
                r = jnp.where(left, lax.rsqrt(ssl / DIL_HEAD_DIM + EPS),
                              lax.rsqrt(ssr / DIL_HEAD_DIM + EPS))
                tn = t * r * w_row_ref[...]
                rot = jnp.where(first, pltpu.roll(tn, LANES - ROPE_DIMS // 2, axis=1),
                                pltpu.roll(tn, ROPE_DIMS // 2, axis=1))
                o_ref[:, j + i * LANES:j + (i + 1) * LANES] = (tn * cm + rot * sm).astype(BF16)

    qk_norm_rope(P_Q, qnw_ref, q_ref)
    qk_norm_rope(P_K, knw_ref, k_ref)

    for h in range(MEM_HEADS):
        t2 = seg(P_MQ + h * MEM_HEAD_PAD, MEM_HEAD_PAD)
        ss = jnp.sum(t2 * t2, axis=-1, keepdims=True)
        mq_ref[:, h * MEM_HEAD_PAD:(h + 1) * MEM_HEAD_PAD] = (
            t2 * lax.rsqrt(ss / MEM_HEAD_DIM + EPS) * mqnw_ref[...]).astype(BF16)


def _in_proj(x2d, pos_b, norm_w, w_packed, qnw, knw, mqnw, invp):
    m, d = x2d.shape
    tm = IN_PROJ_ROWS
    row = lambda width: pl.BlockSpec((tm, width), lambda i: (i, 0))
    outs = [(1024, BF16), (SSD_CONV_DIM, BF16), (LANES, F32), (DIL_WIDTH, BF16), (DIL_WIDTH, BF16),
            (DIL_WIDTH, BF16), (MEM_HEADS * MEM_HEAD_PAD, BF16), (3 * D_MODEL, BF16)]
    return pl.pallas_call(
        _in_proj_kernel,
        grid=(m // tm,),
        in_specs=[row(d), row(LANES), _resident((1, d)), _resident(w_packed.shape),
                  _resident((1, LANES)), _resident((1, LANES)), _resident((1, MEM_HEAD_PAD)),
                  _resident((1, LANES))],
        out_specs=[row(w) for w, _ in outs],
        out_shape=[jax.ShapeDtypeStruct((m, w), dt) for w, dt in outs],
        scratch_shapes=[pltpu.VMEM((tm, d), BF16)],
        compiler_params=pltpu.CompilerParams(vmem_limit_bytes=VMEM_LIMIT_BYTES),
        name="in_proj",
    )(x2d, pos_b, norm_w, w_packed, qnw, knw, mqnw, invp)


def _ssd_kernel(xbc_ref, z_ref, dt_ref, cw_ref, cb_ref, dtb_ref, alog_ref, dfull_ref, nw_ref,
                e_ref, y_ref, tail_ref, state_ref):
    c = pl.program_id(1)
    L = SSD_CHUNK

    @pl.when(c == 0)
    def _():
        tail_ref[...] = jnp.zeros_like(tail_ref)
        state_ref[...] = jnp.zeros_like(state_ref)

    cur = xbc_ref[0].astype(F32)
    xp = jnp.concatenate([tail_ref[...], cur], axis=0)
    conv = cb_ref[...] + cw_ref[SSD_CONV - 1:SSD_CONV, :] * cur
    for k in range(SSD_CONV - 1):
        lo = 8 - (SSD_CONV - 1) + k
        conv = conv + cw_ref[k:k + 1, :] * xp[lo:lo + L]
    tail_ref[...] = cur[L - 8:L]
    xa = conv * _sigmoid(conv)
    xs = xa[:, :D_MODEL]
    gn = SSD_GROUPS * SSD_STATE
    bm = xa[:, D_MODEL:D_MODEL + gn].astype(BF16)
    cm = xa[:, D_MODEL + gn:].astype(BF16)

    pre = dt_ref[0] + dtb_ref[...]
    dt = jnp.maximum(pre, 0.0) + jnp.log(1.0 + jnp.exp(-jnp.abs(pre)))
    da = dt * (-jnp.exp(alog_ref[...]))
    ri = lax.broadcasted_iota(jnp.int32, (L, L), 0)
    ci = lax.broadcasted_iota(jnp.int32, (L, L), 1)
    tril = ri >= ci
    acum = _dot_f32(tril.astype(F32), da)
    acum_t = acum.T
    expand = e_ref[...]
    dt_full = _dot_f32(dt, expand)
    ea_full = _dot_f32(jnp.exp(acum), expand)
    dec_full = _dot_f32(jnp.exp(acum[L - 1:L, :] - acum), expand)
    xdt = xs * dt_full
    xdt_b = xdt.astype(BF16)
    xdd_b = (xdt * dec_full).astype(BF16)
    gw = SSD_HEADS_PER_GROUP * SSD_HEAD_DIM
    head_of_lane = lax.broadcasted_iota(jnp.int32, (L, gw), 1) // SSD_HEAD_DIM
    ys = []
    for g in range(SSD_GROUPS):
        bg = bm[:, g * SSD_STATE:(g + 1) * SSD_STATE]
        cg = cm[:, g * SSD_STATE:(g + 1) * SSD_STATE]
        gs = slice(g * gw, (g + 1) * gw)
        cb = _dot_nt(cg, bg)
        st_old = state_ref[g]
        yg = _dot(cg, st_old.astype(BF16)) * ea_full[:, gs]
        for r in range(SSD_HEADS_PER_GROUP):
            h = g * SSD_HEADS_PER_GROUP + r
            seg = acum[:, h:h + 1] - acum_t[h:h + 1, :]
            lmat = jnp.exp(jnp.where(tril, seg, -jnp.inf))
            yd = _dot((cb * lmat).astype(BF16), xdt_b[:, gs])
            yg = yg + jnp.where(head_of_lane == r, yd, 0.0)
        ys.append(yg)
        s_new = lax.dot_general(bg, xdd_b[:, gs], (((0,), (0,)), ((), ())),
                                preferred_element_type=F32)
        state_ref[g] = st_old * ea_full[L - 1:L, gs] + s_new
    y = jnp.concatenate(ys, axis=1) + dfull_ref[...] * xs
    zf = z_ref[0].astype(F32)
    y = y * (zf * _sigmoid(zf))
    outs = []
    for g in range(SSD_GROUPS):
        yg = y[:, g * gw:(g + 1) * gw]
        ss = jnp.sum(yg * yg, axis=-1, keepdims=True)
        outs.append(yg * lax.rsqrt(ss / gw + EPS))
    y_ref[0] = (jnp.concatenate(outs, axis=1) * nw_ref[...]).astype(BF16)


def _ssd(xbc, z, dt_raw, conv_w, conv_b, dt_bias, a_log, d_full, norm_w, expand):
    b, s, _ = xbc.shape
    L = SSD_CHUNK
    blk = lambda width: pl.BlockSpec((1, L, width), lambda i, c: (i, c, 0))
    return pl.pallas_call(
        _ssd_kernel,
        grid=(b, s // L),
        in_specs=[blk(SSD_CONV_DIM), blk(D_MODEL), blk(LANES),
                  _resident(conv_w.shape), _resident(conv_b.shape), _resident(dt_bias.shape),
                  _resident(a_log.shape), _resident(d_full.shape), _resident(norm_w.shape),
                  _resident(expand.shape)],
        out_specs=blk(D_MODEL),
        out_shape=jax.ShapeDtypeStruct((b, s, D_MODEL), BF16),
        scratch_shapes=[pltpu.VMEM((8, SSD_CONV_DIM), F32),
                        pltpu.VMEM((SSD_GROUPS, SSD_STATE, SSD_HEADS_PER_GROUP * SSD_HEAD_DIM), F32)],
        compiler_params=pltpu.CompilerParams(
            dimension_semantics=("arbitrary", "arbitrary"), vmem_limit_bytes=VMEM_LIMIT_BYTES),
        name="ssd",
    )(xbc, z, dt_raw, conv_w, conv_b, dt_bias, a_log, d_full, norm_w, expand)


DIL_BLOCK = 128
DIL_QROWS = 512


def _dil_attn_kernel(q_ref, kc_ref, kp_ref, vc_ref, vp_ref, o_ref, lse_ref):
    n = pl.program_id(2)
    blk = DIL_BLOCK
    w = DIL_OUT_WIDTH
    head_of_lane = lax.broadcasted_iota(jnp.int32, (blk, w), 1) // DIL_HEAD_DIM
    lane = lax.broadcasted_iota(jnp.int32, (blk, LANES), 1)
    qi = lax.broadcasted_iota(jnp.int32, (blk, 2 * blk), 0)
    ki = lax.broadcasted_iota(jnp.int32, (blk, 2 * blk), 1)
    band = (ki >= qi) & (ki <= qi + blk)
    first_valid = band & (ki + jnp.minimum(n, 1) * blk >= blk)
    for sb in range(DIL_QROWS // blk):
        rows = slice(sb * blk, (sb + 1) * blk)
        q = q_ref[0, rows, :]
        if sb == 0:
            kprev, vprev, valid = kp_ref[0], vp_ref[0], first_valid
        else:
            prev = slice((sb - 1) * blk, sb * blk)
            kprev, vprev, valid = kc_ref[0, prev, :], vc_ref[0, prev, :], band
        kk = jnp.concatenate([kprev, kc_ref[0, rows, :]], axis=0)
        vv = jnp.concatenate([vprev, vc_ref[0, rows, :]], axis=0)
        o_acc = jnp.zeros((blk, w), F32)
        lse_t = jnp.zeros((blk, LANES), F32)
        for h in range(w // DIL_HEAD_DIM):
            qm = jnp.where(head_of_lane == h, q, jnp.zeros_like(q))
            s = _dot_nt(qm, kk) * (DIL_HEAD_DIM ** -0.5)
            s = jnp.where(valid, s, -jnp.inf)
            m = jnp.max(s, axis=-1, keepdims=True)
            p = jnp.exp(s - m)
            l = jnp.sum(p, axis=-1, keepdims=True)
            of = _dot(p.astype(BF16), vv)
            o_acc = jnp.where(head_of_lane == h, of / l, o_acc)
            lse_t = jnp.where(lane == h, m + jnp.log(l), lse_t)
        o_ref[0, rows, :] = o_acc.astype(BF16)
        lse_ref[0, rows, :] = lse_t


def _dil_attn(q3, k3, v3, gi, dilation):
    b, m, _ = q3.shape
    ng = len(DIL_PAIRS)
    qr = DIL_QROWS
    sub = qr // DIL_BLOCK
    cur = pl.BlockSpec((1, qr, DIL_OUT_WIDTH), lambda i, r, n: (i, n, r * ng + gi))
    prev = pl.BlockSpec((1, DIL_BLOCK, DIL_OUT_WIDTH),
                        lambda i, r, n: (i, jnp.maximum(n * sub - 1, 0), r * ng + gi))
    return pl.pallas_call(
        _dil_attn_kernel,
        grid=(b, dilation, m // qr),
        in_specs=[cur, cur, prev, cur, prev],
        out_specs=[pl.BlockSpec((1, qr, DIL_OUT_WIDTH), lambda i, r, n: (i, n, r)),
                   pl.BlockSpec((1, qr, LANES), lambda i, r, n: (i, n, r))],
        out_shape=[jax.ShapeDtypeStruct((b, m, dilation * DIL_OUT_WIDTH), BF16),
                   jax.ShapeDtypeStruct((b, m, dilation * LANES), F32)],
        compiler_params=pltpu.CompilerParams(vmem_limit_bytes=VMEM_LIMIT_BYTES),
        name=f"dil_attn_g{gi}",
    )(q3, k3, k3, v3, v3)


def _mem_kv_kernel(mem_ref, nw_ref, w_ref, knw_ref, k_ref, v_ref):
    x = mem_ref[0]
    ms = jnp.mean(x * x, axis=-1, keepdims=True)
    u = (x * lax.rsqrt(ms + EPS) * nw_ref[...]).astype(BF16)
    width = MEM_HEADS * MEM_HEAD_PAD
    for h in range(MEM_HEADS):
        hs = slice(h * MEM_HEAD_PAD, (h + 1) * MEM_HEAD_PAD)
        kh = _dot(u, w_ref[:, hs])
        ss = jnp.sum(kh * kh, axis=-1, keepdims=True)
        k_ref[0, :, hs] = (kh * lax.rsqrt(ss / MEM_HEAD_DIM + EPS) * knw_ref[...]).astype(BF16)
        vs = slice(width + h * MEM_HEAD_PAD, width + (h + 1) * MEM_HEAD_PAD)
        v_ref[0, :, hs] = _dot(u, w_ref[:, vs]).astype(BF16)


def _mem_kv(mem, norm_w, w_kv, knw):
    b, n_mem, d = mem.shape
    width = MEM_HEADS * MEM_HEAD_PAD
    blk = pl.BlockSpec((1, n_mem, width), lambda i: (i, 0, 0))
    return pl.pallas_call(
        _mem_kv_kernel,
        grid=(b,),
        in_specs=[pl.BlockSpec((1, n_mem, d), lambda i: (i, 0, 0)), _resident((1, d)),
                  _resident(w_kv.shape), _resident((1, MEM_HEAD_PAD))],
        out_specs=[blk, blk],
        out_shape=[jax.ShapeDtypeStruct((b, n_mem, width), BF16)] * 2,
        compiler_params=pltpu.CompilerParams(vmem_limit_bytes=VMEM_LIMIT_BYTES),
        name="mem_kv",
    )(mem, norm_w, w_kv, knw)


MEM_QROWS = 512


def _mem_attn_kernel(q_ref, k_ref, v_ref, o_ref):
    for h in range(MEM_HEADS):
        hs = slice(h * MEM_HEAD_PAD, (h + 1) * MEM_HEAD_PAD)
        s = _dot_nt(q_ref[0, :, hs], k_ref[0, :, hs]) * (MEM_HEAD_DIM ** -0.5)
        m = jnp.max(s, axis=-1, keepdims=True)
        p = jnp.exp(s - m)
        l = jnp.sum(p, axis=-1, keepdims=True)
        o_ref[0, :, hs] = (_dot(p.astype(BF16), v_ref[0, :, hs]) / l).astype(BF16)


def _mem_attn(q, k, v):
    b, s, width = q.shape
    n_mem = k.shape[1]
    tm = MEM_QROWS
    kv = pl.BlockSpec((1, n_mem, width), lambda i, j: (i, 0, 0))
    return pl.pallas_call(
        _mem_attn_kernel,
        grid=(b, s // tm),
        in_specs=[pl.BlockSpec((1, tm, width), lambda i, j: (i, j, 0)), kv, kv],
        out_specs=pl.BlockSpec((1, tm, width), lambda i, j: (i, j, 0)),
        out_shape=jax.ShapeDtypeStruct((b, s, width), BF16),
        compiler_params=pltpu.CompilerParams(vmem_limit_bytes=VMEM_LIMIT_BYTES),
        name="mem_attn",
    )(q, k, v)


MERGE_ROWS = 512
MERGE_SLAB = 256


def _merge_kernel(x_ref, yssd_ref, o0_ref, o1_ref, o2_ref, l0_ref, l1_ref, l2_ref, ymem_ref,
                  gate_ref, wssd_ref, wdil_ref, wmem_ref, wout_ref, e4_ref, h_ref):
    for r0 in range(0, MERGE_ROWS, MERGE_SLAB):
        rows = slice(r0, r0 + MERGE_SLAB)
        l0, l1, l2 = l0_ref[rows, :], l1_ref[rows, :], l2_ref[rows, :]
        lmax = jnp.maximum(jnp.maximum(l0, l1), l2)
        e0, e1, e2 = jnp.exp(l0 - lmax), jnp.exp(l1 - lmax), jnp.exp(l2 - lmax)
        inv = 1.0 / (e0 + e1 + e2)
        ydil = jnp.zeros((MERGE_SLAB, DIL_OUT_WIDTH), F32)
        for e, o_ref in ((e0, o0_ref), (e1, o1_ref), (e2, o2_ref)):
            ydil = ydil + _dot_f32(e * inv, e4_ref[...]) * o_ref[rows, :].astype(F32)
        merged = (_sigmoid(gate_ref[rows, 0:D_MODEL].astype(F32))
                  * _dot(yssd_ref[rows, :], wssd_ref[...])
                  + _sigmoid(gate_ref[rows, D_MODEL:2 * D_MODEL].astype(F32))
                  * _dot(ydil.astype(BF16), wdil_ref[...])
                  + _sigmoid(gate_ref[rows, 2 * D_MODEL:3 * D_MODEL].astype(F32))
                  * _dot(ymem_ref[rows, :], wmem_ref[...]))
        h_ref[rows, :] = x_ref[rows, :] + _dot(merged.astype(BF16), wout_ref[...])


def _merge(x2d, yssd, o, lse, ymem, gate, wssd, wdil, wmem, wout, e4):
    m, d = x2d.shape
    tm = MERGE_ROWS
    row = lambda width: pl.BlockSpec((tm, width), lambda i: (i, 0))
    return pl.pallas_call(
        _merge_kernel,
        grid=(m // tm,),
        in_specs=[row(d), row(d), row(DIL_OUT_WIDTH), row(DIL_OUT_WIDTH), row(DIL_OUT_WIDTH),
                  row(LANES), row(LANES), row(LANES), row(MEM_HEADS * MEM_HEAD_PAD), row(3 * d),
                  _resident(wssd.shape), _resident(wdil.shape), _resident(wmem.shape),
                  _resident(wout.shape), _resident(e4.shape)],
        out_specs=row(d),
        out_shape=jax.ShapeDtypeStruct((m, d), F32),
        compiler_params=pltpu.CompilerParams(vmem_limit_bytes=VMEM_LIMIT_BYTES),
        name="merge",
    )(x2d, yssd, o[0], o[1], o[2], lse[0], lse[1], lse[2], ymem, gate, wssd, wdil, wmem, wout, e4)


def _pad_heads(w, axis):
    shape = list(w.shape)
    shape[axis:axis + 1] = [MEM_HEADS, MEM_HEAD_DIM]
    w = w.reshape(shape)
    pad = [(0, 0)] * w.ndim
    pad[axis + 1] = (0, MEM_HEAD_PAD - MEM_HEAD_DIM)
    w = jnp.pad(w, pad)
    shape[axis:axis + 2] = [MEM_HEADS * MEM_HEAD_PAD]
    return w.reshape(shape)


def _mix_block(x, mem, positions, norm_mix_w, w_in, ssd_conv_w, ssd_conv_b, ssd_dt_bias,
               ssd_a_log, ssd_d, ssd_norm_w, dil_q_norm_w, dil_k_norm_w, mem_norm_w, w_mem_kv,
               mem_q_norm_w, mem_k_norm_w, w_ssd_br, w_dil_br, w_mem_br, w_out):
    b, s, d = x.shape
    m = b * s
    pad_lanes = lambda v: jnp.pad(v, (0, LANES - v.shape[0])).reshape(1, LANES)
    w_packed = jnp.concatenate([
        w_in[:, OFF_Z:OFF_DT],
        jnp.pad(w_in[:, OFF_DT:OFF_DQ], ((0, 0), (0, LANES - SSD_HEADS))),
        w_in[:, OFF_DQ:OFF_MQ],
        _pad_heads(w_in[:, OFF_MQ:OFF_GATE], 1),
        w_in[:, OFF_GATE:]], axis=1).astype(BF16)
    half = ROPE_DIMS // 2
    inv = jnp.exp(-math.log(ROPE_THETA) * (2.0 / ROPE_DIMS) * jnp.arange(half, dtype=F32))
    invp = jnp.tile(inv, LANES // half).reshape(1, LANES)
    pos_b = jnp.broadcast_to(positions.astype(F32).reshape(m, 1), (m, LANES))
    tile2 = lambda v: jnp.tile(v, 2).reshape(1, LANES)
    mqnw = jnp.pad(mem_q_norm_w, (0, MEM_HEAD_PAD - MEM_HEAD_DIM)).reshape(1, MEM_HEAD_PAD)
    mknw = jnp.pad(mem_k_norm_w, (0, MEM_HEAD_PAD - MEM_HEAD_DIM)).reshape(1, MEM_HEAD_PAD)
    z, xbc, dt_raw, q, k, v, mq, gate = _in_proj(
        x.reshape(m, d), pos_b, norm_mix_w.reshape(1, d), w_packed,
        tile2(dil_q_norm_w), tile2(dil_k_norm_w), mqnw, invp)

    expand = jnp.repeat(jnp.eye(LANES, SSD_HEADS, dtype=F32), SSD_HEAD_DIM, axis=1)
    y_ssd = _ssd(xbc.reshape(b, s, -1), z.reshape(b, s, -1), dt_raw.reshape(b, s, -1),
                 ssd_conv_w, ssd_conv_b.reshape(1, -1), pad_lanes(ssd_dt_bias), pad_lanes(ssd_a_log),
                 jnp.repeat(ssd_d, SSD_HEAD_DIM).reshape(1, d), ssd_norm_w.reshape(1, d), expand)

    outs, lses = [], []
    for gi, (window, dilation) in enumerate(DIL_PAIRS):
        assert window // dilation == DIL_BLOCK
        view = lambda t: t.reshape(b, s // dilation, dilation * DIL_WIDTH)
        o, lse = _dil_attn(view(q), view(k), view(v), gi, dilation)
        outs.append(o.reshape(m, DIL_OUT_WIDTH))
        lses.append(lse.reshape(m, LANES))

    w_kv = jnp.concatenate([_pad_heads(w_mem_kv[:, :MEM_WIDTH], 1),
                            _pad_heads(w_mem_kv[:, MEM_WIDTH:], 1)], axis=1).astype(BF16)
    mk, mv = _mem_kv(mem, mem_norm_w.reshape(1, d), w_kv, mknw)
    y_mem = _mem_attn(mq.reshape(b, s, -1), mk, mv).reshape(m, -1)

    e4 = jnp.repeat(jnp.eye(LANES, DIL_OUT_WIDTH // DIL_HEAD_DIM, dtype=F32), DIL_HEAD_DIM, axis=1)
    return _merge(x.reshape(m, d), y_ssd.reshape(m, d), outs, lses, y_mem, gate,
                  w_ssd_br.astype(BF16), w_dil_br.astype(BF16),
                  _pad_heads(w_mem_br, 0).astype(BF16), w_out.astype(BF16), e4)


PEER_TOKENS = 512
PEER_A_PER_STEP = 4


def _oddeven_merge(lo, hi, r):
    step = r * 2
    if step < hi - lo:
        yield from _oddeven_merge(lo, hi, step)
        yield from _oddeven_merge(lo + r, hi, step)
        yield from [(i, i + r) for i in range(lo + r, hi - r, step)]
    else:
        yield (lo, lo + r)


def _oddeven_merge_sort(lo, hi):
    if hi - lo >= 1:
        mid = lo + (hi - lo) // 2
        yield from _oddeven_merge_sort(lo, mid)
        yield from _oddeven_merge_sort(mid + 1, hi)
        yield from _oddeven_merge(lo, hi, 1)


_SORT16 = tuple(_oddeven_merge_sort(0, PEER_TOPK - 1))


def _cmpx(a, i, j):
    hi = jnp.maximum(a[i], a[j])
    lo = jnp.minimum(a[i], a[j])
    a[i], a[j] = hi, lo


def _sort16_desc(a):
    for i, j in _SORT16:
        _cmpx(a, i, j)
    return a


def _bitonic16_desc(a):
    for d in (8, 4, 2, 1):
        for i in range(PEER_TOPK):
            if not i & d:
                _cmpx(a, i, i + d)
    return a


def _merge_top16(a, b):
    c = [jnp.maximum(a[k], b[PEER_TOPK - 1 - k]) for k in range(PEER_TOPK)]
    return _bitonic16_desc(c)


def _merge_sublanes(a):
    for sh in (4, 2, 1):
        b = [pltpu.roll(x, sh, axis=0) for x in a]
        a = _merge_top16(a, b)
    return a


def _top16_rows(s):
    rows = [s[8 * i:8 * i + 8, :] for i in range(PEER_N_KEYS // 8)]
    return _merge_sublanes(_sort16_desc(rows))


def _peer_route_kernel(h_ref, nw_ref, wqt_ref, keys_ref,
                       hnt_ref, theta_ref, f1_ref, s2_ref, e2_ref):
    t = h_ref.shape[0]
    x = h_ref[...]
    ms = jnp.mean(x * x, axis=-1, keepdims=True)
    hn = x * lax.rsqrt(ms + EPS) * nw_ref[...]
    hnt = hn.T.astype(BF16)
    hnt_ref[...] = hnt
    sub = lax.broadcasted_iota(jnp.int32, (8, t), 0)

    def distribute(vs):
        out = vs[7]
        for r in range(6, -1, -1):
            out = jnp.where(sub == r, vs[r], out)
        return out

    for h in range(PEER_HEADS):
        qt = jnp.dot(wqt_ref[h * PEER_QUERY_DIM:(h + 1) * PEER_QUERY_DIM, :], hnt,
                     preferred_element_type=F32).astype(BF16)
        half = PEER_QUERY_DIM // 2
        s1 = jnp.dot(keys_ref[h, 0], qt[:half], preferred_element_type=F32)
        s2 = jnp.dot(keys_ref[h, 1], qt[half:], preferred_element_type=F32)
        v1 = _top16_rows(s1)
        v2 = _top16_rows(s2)
        a_lo = distribute(v1[:8])
        a_hi = distribute(v1[8:])
        lo = [a_lo + v2[j] for j in range(PEER_TOPK)]
        hi = [a_hi + v2[j] for j in range(PEER_TOPK)]
        tau = _merge_sublanes(_merge_top16(lo, hi))[PEER_TOPK - 1]
        e_lo = jnp.exp(a_lo - v1[0])
        e_hi = jnp.exp(a_hi - v1[0])
        zp = jnp.zeros((8, t), F32)
        for j in range(PEER_TOPK):
            e2j = jnp.exp(v2[j] - v2[0])
            zp = zp + jnp.where(lo[j] >= tau, e_lo * e2j, 0.0)
            zp = zp + jnp.where(hi[j] >= tau, e_hi * e2j, 0.0)
        z = jnp.sum(zp, axis=0, keepdims=True)
        tau_r = tau[0:1, :]
        theta = jnp.full(s1.shape, jnp.inf, F32)
        for j in range(PEER_TOPK):
            v2j = v2[j][0:1, :]
            theta = jnp.where(s1 + v2j >= tau_r, v2j, theta)
        theta_ref[h] = theta
        f1_ref[h] = jnp.exp(s1 - v1[0][0:1, :]) / z
        s2_ref[h] = s2
        e2_ref[h] = jnp.exp(s2 - v2[0][0:1, :])


def _peer_route(h2d, norm_w, wqt, keys):
    n, d = h2d.shape
    t = PEER_TOKENS
    route = jax.ShapeDtypeStruct((PEER_HEADS, PEER_N_KEYS, n), F32)
    route_spec = pl.BlockSpec((PEER_HEADS, PEER_N_KEYS, t), lambda i: (0, 0, i))
    return pl.pallas_call(
        _peer_route_kernel,
        grid=(n // t,),
        in_specs=[
            pl.BlockSpec((t, d), lambda i: (i, 0)),
            pl.BlockSpec((1, d), lambda i: (0, 0)),
            pl.BlockSpec(wqt.shape, lambda i: (0, 0)),
            pl.BlockSpec(keys.shape, lambda i: (0, 0, 0, 0)),
        ],
        out_specs=[pl.BlockSpec((d, t), lambda i: (0, i)),
                   route_spec, route_spec, route_spec, route_spec],
        out_shape=[jax.ShapeDtypeStruct((d, n), BF16), route, route, route, route],
        compiler_params=pltpu.CompilerParams(vmem_limit_bytes=VMEM_LIMIT_BYTES),
        name="peer_route",
    )(h2d, norm_w.reshape(1, d), wqt, keys)


def _peer_expert_kernel(h_ref, hnt_ref, theta_ref, f1_ref, s2_ref, e2_ref,
                        down_ref, upt_ref, o_ref, acc_ref):
    step = pl.program_id(1)

    @pl.when(step == 0)
    def _():
        acc_ref[...] = jnp.zeros_like(acc_ref)

    s = jnp.dot(down_ref[...], hnt_ref[...], preferred_element_type=F32)
    act = 0.5 * s * (1.0 + lax.erf(s * (2.0 ** -0.5)))
    coefs = []
    for q in range(PEER_A_PER_STEP):
        a = step * PEER_A_PER_STEP + q
        w = None
        for h in range(PEER_HEADS):
            th = theta_ref[h, pl.ds(a, 1), :]
            f1 = f1_ref[h, pl.ds(a, 1), :]
            wh = jnp.where(s2_ref[h] >= th, e2_ref[h], 0.0) * f1
            w = wh if w is None else w + wh
        coefs.append((w * act[q * PEER_N_KEYS:(q + 1) * PEER_N_KEYS, :]).astype(BF16))
    coef = jnp.concatenate(coefs, axis=0)
    acc_ref[...] += jnp.dot(upt_ref[...], coef, preferred_element_type=F32)

    @pl.when(step == pl.num_programs(1) - 1)
    def _():
        o_ref[...] = h_ref[...] + acc_ref[...].T


def _peer_expert(h2d, hnt, theta, f1, s2, e2, down_bf, upt_bf):
    n, d = h2d.shape
    t = PEER_TOKENS
    eb = PEER_A_PER_STEP * PEER_N_KEYS
    route_spec = pl.BlockSpec((PEER_HEADS, PEER_N_KEYS, t), lambda i, j: (0, 0, i))
    return pl.pallas_call(
        _peer_expert_kernel,
        grid=(n // t, PEER_N_KEYS // PEER_A_PER_STEP),
        in_specs=[
            pl.BlockSpec((t, d), lambda i, j: (i, 0)),
            pl.BlockSpec((d, t), lambda i, j: (0, i)),
            route_spec, route_spec, route_spec, route_spec,
            pl.BlockSpec((eb, d), lambda i, j: (j, 0)),
            pl.BlockSpec((d, eb), lambda i, j: (0, j)),
        ],
        out_specs=pl.BlockSpec((t, d), lambda i, j: (i, 0)),
        out_shape=jax.ShapeDtypeStruct((n, d), F32),
        scratch_shapes=[pltpu.VMEM((d, t), F32)],
        compiler_params=pltpu.CompilerParams(
            dimension_semantics=("arbitrary", "arbitrary"),
            vmem_limit_bytes=VMEM_LIMIT_BYTES),
        name="peer_expert",
    )(h2d, hnt, theta, f1, s2, e2, down_bf, upt_bf)


def _peer_block(h2d, norm_w, w_query, sub_keys, expert_down, expert_up):
    wqt = w_query.T.astype(BF16)
    hnt, theta, f1, s2, e2 = _peer_route(h2d, norm_w, wqt, sub_keys.astype(BF16))
    return _peer_expert(h2d, hnt, theta, f1, s2, e2,
                        expert_down.astype(BF16), expert_up.T.astype(BF16))


def kernel(x, mem, positions, norm_mix_w, w_in, ssd_conv_w, ssd_conv_b, ssd_dt_bias,
           ssd_a_log, ssd_d, ssd_norm_w, dil_q_norm_w, dil_k_norm_w, mem_norm_w, w_mem_kv,
           mem_q_norm_w, mem_k_norm_w, w_ssd_br, w_dil_br, w_mem_br, w_out, norm_ffn_w,
           peer_w_query, peer_sub_keys, peer_down, peer_up):
    b, s, d = x.shape
    assert norm_mix_w.shape[0] == 1, "single-layer block"
    h = _mix_block(x, mem, positions, norm_mix_w[0], w_in[0], ssd_conv_w[0], ssd_conv_b[0],
                   ssd_dt_bias[0], ssd_a_log[0], ssd_d[0], ssd_norm_w[0], dil_q_norm_w[0],
                   dil_k_norm_w[0], mem_norm_w[0], w_mem_kv[0], mem_q_norm_w[0], mem_k_norm_w[0],
                   w_ssd_br[0], w_dil_br[0], w_mem_br[0], w_out[0])
    h = _peer_block(h, norm_ffn_w[0], peer_w_query[0], peer_sub_keys[0], peer_down[0], peer_up[0])
    return h.reshape(b, s, d)
```

```python
import math

import jax
import jax.numpy as jnp
from jax import lax
from jax.experimental import pallas as pl
from jax.experimental.pallas import tpu as pltpu

D_MODEL = 1024
EPS = 1e-6
SSD_HEAD_DIM = 64
SSD_HEADS = 16
SSD_GROUPS = 4
SSD_HEADS_PER_GROUP = 4
SSD_STATE = 128
SSD_CONV = 4
SSD_CHUNK = 128
SSD_CONV_DIM = 2048
DIL_PAIRS = ((128, 1), (512, 4), (2048, 16))
DIL_HEAD_DIM = 64
DIL_WIDTH = 768
DIL_OUT_WIDTH = 256
MEM_HEADS = 4
MEM_HEAD_DIM = 192
MEM_HEAD_PAD = 256
MEM_WIDTH = 768
ROPE_THETA = 500000.0
ROPE_DIMS = 16
PEER_HEADS = 8
PEER_N_KEYS = 128
PEER_QUERY_DIM = 256
PEER_TOPK = 16
OFF_Z = 0
OFF_XBC = 1024
OFF_DT = 3072
OFF_DQ = 3088
OFF_DK = OFF_DQ + DIL_WIDTH
OFF_DV = OFF_DK + DIL_WIDTH
OFF_MQ = OFF_DV + DIL_WIDTH
OFF_GATE = OFF_MQ + MEM_WIDTH
F32 = jnp.float32
BF16 = jnp.bfloat16
HIGHEST = lax.Precision.HIGHEST
LANES = 128
VMEM_LIMIT_BYTES = 56 * 1024 * 1024

P_Z = 0
P_XBC = 1024
P_DT = 3072
P_Q = 3200
P_K = P_Q + DIL_WIDTH
P_V = P_K + DIL_WIDTH
P_MQ = P_V + DIL_WIDTH
P_GATE = P_MQ + MEM_HEADS * MEM_HEAD_PAD
P_WIDTH = P_GATE + 3 * D_MODEL


def _resident(shape):
    nd = len(shape)
    return pl.BlockSpec(shape, lambda *_: (0,) * nd, pipeline_mode=pl.Buffered(1))


def _sigmoid(x):
    return 1.0 / (1.0 + jnp.exp(-x))


def _dot(a, b):
    return jnp.dot(a, b, preferred_element_type=F32)


def _dot_nt(a, b):
    return lax.dot_general(a, b, (((1,), (1,)), ((), ())), preferred_element_type=F32)


def _dot_f32(a, b):
    return jnp.dot(a, b, precision=HIGHEST, preferred_element_type=F32)


IN_PROJ_ROWS = 512


def _in_proj_kernel(x_ref, pos_ref, nw_ref, w_ref, qnw_ref, knw_ref, mqnw_ref, invp_ref,
                    z_ref, xbc_ref, dt_ref, q0_ref, q1_ref, q2_ref, k0_ref, k1_ref, k2_ref,
                    v0_ref, v1_ref, v2_ref, mq_ref, gate_ref, u_ref, stage_a_ref, stage_b_ref):
    x = x_ref[...]
    ms = jnp.mean(x * x, axis=-1, keepdims=True)
    u_ref[...] = (x * lax.rsqrt(ms + EPS) * nw_ref[...]).astype(BF16)

    def seg(off, width):
        return _dot(u_ref[...], w_ref[:, off:off + width])

    for j in range(0, 1024, 256):
        z_ref[:, j:j + 256] = seg(P_Z + j, 256).astype(BF16)
    for j in range(0, SSD_CONV_DIM, 256):
        xbc_ref[:, j:j + 256] = seg(P_XBC + j, 256).astype(BF16)
    dt_ref[...] = seg(P_DT, LANES)
    rows = x.shape[0]

    stages = (stage_a_ref, stage_b_ref)

    def put_group(gi, o_ref):
        dil = DIL_PAIRS[gi][1]
        for i, st in enumerate(stages):
            for r in range(dil):
                tok = pl.ds(r, rows // dil, stride=dil) if dil > 1 else slice(None)
                lo = r * DIL_OUT_WIDTH + i * LANES
                o_ref[:, lo:lo + LANES] = st[tok, :].astype(BF16)

    for gi, o_ref in enumerate((v0_ref, v1_ref, v2_ref)):
        vt = seg(P_V + gi * DIL_OUT_WIDTH, DIL_OUT_WIDTH)
        for i, st in enumerate(stages):
            st[...] = vt[:, i * LANES:(i + 1) * LANES]
        put_group(gi, o_ref)
    for j in range(0, 3 * D_MODEL, 256):
        gate_ref[:, j:j + 256] = seg(P_GATE + j, 256).astype(BF16)

    ang = pos_ref[...] * invp_ref[...]
    cos = jnp.cos(ang)
    sin = jnp.sin(ang)
    lane = lax.broadcasted_iota(jnp.int32, (rows, LANES), 1)
    l64 = lane & (DIL_HEAD_DIM - 1)
    first = l64 < ROPE_DIMS // 2
    rotated = l64 < ROPE_DIMS
    cm = jnp.where(rotated, cos, 1.0)
    sm = jnp.where(rotated, jnp.where(first, -sin, sin), 0.0)
    left = lane < DIL_HEAD_DIM

    def qk_norm_rope(off, w_row_ref, o_refs):
        for gi, o_ref in enumerate(o_refs):
            t2 = seg(off + gi * DIL_OUT_WIDTH, DIL_OUT_WIDTH)
            for i in range(2):
                t = t2[:, i * LANES:(i + 1) * LANES]
                sq = t * t
                ssl = jnp.sum(jnp.where(left, sq, 0.0), axis=-1, keepdims=True)
                ssr = jnp.sum(jnp.where(left, 0.0, sq), axis=-1, keepdims=True)
                r = jnp.where(left, lax.rsqrt(ssl / DIL_HEAD_DIM + EPS),
                              lax.rsqrt(ssr / DIL_HEAD_DIM + EPS))
                tn = t * r * w_row_ref[...]
                rot = jnp.where(first, pltpu.roll(tn, LANES - ROPE_DIMS // 2, axis=1),
                                pltpu.roll(tn, ROPE_DIMS // 2, axis=1))
                stages[i][...] = tn * cm + rot * sm
            put_group(gi, o_ref)

    qk_norm_rope(P_Q, qnw_ref, (q0_ref, q1_ref, q2_ref))
    qk_norm_rope(P_K, knw_ref, (k0_ref, k1_ref, k2_ref))

    for h in range(MEM_HEADS):
        t2 = seg(P_MQ + h * MEM_HEAD_PAD, MEM_HEAD_PAD)
        ss = jnp.sum(t2 * t2, axis=-1, keepdims=True)
        mq_ref[:, h * MEM_HEAD_PAD:(h + 1) * MEM_HEAD_PAD] = (
            t2 * lax.rsqrt(ss / MEM_HEAD_DIM + EPS) * mqnw_ref[...]).astype(BF16)


def _in_proj(x2d, pos_b, norm_w, w_packed, qnw, knw, mqnw, invp):
    m, d = x2d.shape
    tm = IN_PROJ_ROWS
    row = lambda width, dil=1: pl.BlockSpec((tm // dil, width * dil), lambda i: (i, 0))
    dil_outs = [(DIL_OUT_WIDTH, BF16, dil) for _, dil in DIL_PAIRS]
    outs = ([(1024, BF16, 1), (SSD_CONV_DIM, BF16, 1), (LANES, F32, 1)] + dil_outs * 3
            + [(MEM_HEADS * MEM_HEAD_PAD, BF16, 1), (3 * D_MODEL, BF16, 1)])
    return pl.pallas_call(
        _in_proj_kernel,
        grid=(m // tm,),
        in_specs=[row(d), row(LANES), _resident((1, d)), _resident(w_packed.shape),
                  _resident((1, LANES)), _resident((1, LANES)), _resident((1, MEM_HEAD_PAD)),
                  _resident((1, LANES))],
        out_specs=[row(w, dil) for w, _, dil in outs],
        out_shape=[jax.ShapeDtypeStruct((m // dil, w * dil), dt) for w, dt, dil in outs],
        scratch_shapes=[pltpu.VMEM((tm, d), BF16), pltpu.VMEM((tm, LANES), F32),
                        pltpu.VMEM((tm, LANES), F32)],
        compiler_params=pltpu.CompilerParams(vmem_limit_bytes=VMEM_LIMIT_BYTES),
        name="in_proj",
    )(x2d, pos_b, norm_w, w_packed, qnw, knw, mqnw, invp)


def _ssd_kernel(xbc_ref, z_ref, dt_ref, cw_ref, cb_ref, dtb_ref, alog_ref, dfull_ref, nw_ref,
                e_ref, y_ref, tail_ref, state_ref):
    c = pl.program_id(1)
    L = SSD_CHUNK

    @pl.when(c == 0)
    def _():
        tail_ref[...] = jnp.zeros_like(tail_ref)
        state_ref[...] = jnp.zeros_like(state_ref)

    cur = xbc_ref[0].astype(F32)
    xp = jnp.concatenate([tail_ref[...], cur], axis=0)
    conv = cb_ref[...] + cw_ref[SSD_CONV - 1:SSD_CONV, :] * cur
    for k in range(SSD_CONV - 1):
        lo = 8 - (SSD_CONV - 1) + k
        conv = conv + cw_ref[k:k + 1, :] * xp[lo:lo + L]
    tail_ref[...] = cur[L - 8:L]
    xa = conv * _sigmoid(conv)
    xs = xa[:, :D_MODEL]
    gn = SSD_GROUPS * SSD_STATE
    bm = xa[:, D_MODEL:D_MODEL + gn].astype(BF16)
    cm = xa[:, D_MODEL + gn:].astype(BF16)

    pre = dt_ref[0] + dtb_ref[...]
    dt = jnp.maximum(pre, 0.0) + jnp.log(1.0 + jnp.exp(-jnp.abs(pre)))
    da = dt * (-jnp.exp(alog_ref[...]))
    ri = lax.broadcasted_iota(jnp.int32, (L, L), 0)
    ci = lax.broadcasted_iota(jnp.int32, (L, L), 1)
    tril = ri >= ci
    acum = _dot_f32(tril.astype(F32), da)
    acum_t = acum.T
    expand = e_ref[...]
    dt_full = _dot_f32(dt, expand)
    ea_full = _dot_f32(jnp.exp(acum), expand)
    dec_full = _dot_f32(jnp.exp(acum[L - 1:L, :] - acum), expand)
    xdt = xs * dt_full
    xdt_b = xdt.astype(BF16)
    xdd_b = (xdt * dec_full).astype(BF16)
    gw = SSD_HEADS_PER_GROUP * SSD_HEAD_DIM
    head_of_lane = lax.broadcasted_iota(jnp.int32, (L, gw), 1) // SSD_HEAD_DIM
    ys = []
    for g in range(SSD_GROUPS):
        bg = bm[:, g * SSD_STATE:(g + 1) * SSD_STATE]
        cg = cm[:, g * SSD_STATE:(g + 1) * SSD_STATE]
        gs = slice(g * gw, (g + 1) * gw)
        cb = _dot_nt(cg, bg)
        st_old = state_ref[g]
        yg = _dot(cg, st_old.astype(BF16)) * ea_full[:, gs]
        for r in range(SSD_HEADS_PER_GROUP):
            h = g * SSD_HEADS_PER_GROUP + r
            seg = acum[:, h:h + 1] - acum_t[h:h + 1, :]
            lmat = jnp.exp(jnp.where(tril, seg, -jnp.inf))
            yd = _dot((cb * lmat).astype(BF16), xdt_b[:, gs])
            yg = yg + jnp.where(head_of_lane == r, yd, 0.0)
        ys.append(yg)
        s_new = lax.dot_general(bg, xdd_b[:, gs], (((0,), (0,)), ((), ())),
                                preferred_element_type=F32)
        state_ref[g] = st_old * ea_full[L - 1:L, gs] + s_new
    y = jnp.concatenate(ys, axis=1) + dfull_ref[...] * xs
    zf = z_ref[0].astype(F32)
    y = y * (zf * _sigmoid(zf))
    outs = []
    for g in range(SSD_GROUPS):
        yg = y[:, g * gw:(g + 1) * gw]
        ss = jnp.sum(yg * yg, axis=-1, keepdims=True)
        outs.append(yg * lax.rsqrt(ss / gw + EPS))
    y_ref[0] = (jnp.concatenate(outs, axis=1) * nw_ref[...]).astype(BF16)


def _ssd(xbc, z, dt_raw, conv_w, conv_b, dt_bias, a_log, d_full, norm_w, expand):
    b, s, _ = xbc.shape
    L = SSD_CHUNK
    blk = lambda width: pl.BlockSpec((1, L, width), lambda i, c: (i, c, 0))
    return pl.pallas_call(
        _ssd_kernel,
        grid=(b, s // L),
        in_specs=[blk(SSD_CONV_DIM), blk(D_MODEL), blk(LANES),
                  _resident(conv_w.shape), _resident(conv_b.shape), _resident(dt_bias.shape),
                  _resident(a_log.shape), _resident(d_full.shape), _resident(norm_w.shape),
                  _resident(expand.shape)],
        out_specs=blk(D_MODEL),
        out_shape=jax.ShapeDtypeStruct((b, s, D_MODEL), BF16),
        scratch_shapes=[pltpu.VMEM((8, SSD_CONV_DIM), F32),
                        pltpu.VMEM((SSD_GROUPS, SSD_STATE, SSD_HEADS_PER_GROUP * SSD_HEAD_DIM), F32)],
        compiler_params=pltpu.CompilerParams(
            dimension_semantics=("arbitrary", "arbitrary"), vmem_limit_bytes=VMEM_LIMIT_BYTES),
        name="ssd",
    )(xbc, z, dt_raw, conv_w, conv_b, dt_bias, a_log, d_full, norm_w, expand)


DIL_BLOCK = 128
DIL_QROWS = 512


def _dil_attn_kernel(q_ref, kc_ref, kp_ref, vc_ref, vp_ref, o_ref, lse_ref):
    n = pl.program_id(2)
    blk = DIL_BLOCK
    w = DIL_OUT_WIDTH
    head_of_lane = lax.broadcasted_iota(jnp.int32, (blk, w), 1) // DIL_HEAD_DIM
    lane = lax.broadcasted_iota(jnp.int32, (blk, LANES), 1)
    qi = lax.broadcasted_iota(jnp.int32, (blk, 2 * blk), 0)
    ki = lax.broadcasted_iota(jnp.int32, (blk, 2 * blk), 1)
    band = (ki >= qi) & (ki <= qi + blk)
    first_valid = band & (ki + jnp.minimum(n, 1) * blk >= blk)
    for sb in range(DIL_QROWS // blk):
        rows = slice(sb * blk, (sb + 1) * blk)
        q = q_ref[0, rows, :]
        if sb == 0:
            kprev, vprev, valid = kp_ref[0], vp_ref[0], first_valid
        else:
            prev = slice((sb - 1) * blk, sb * blk)
            kprev, vprev, valid = kc_ref[0, prev, :], vc_ref[0, prev, :], band
        kk = jnp.concatenate([kprev, kc_ref[0, rows, :]], axis=0)
        vv = jnp.concatenate([vprev, vc_ref[0, rows, :]], axis=0)
        o_acc = jnp.zeros((blk, w), F32)
        lse_t = jnp.zeros((blk, LANES), F32)
        for h in range(w // DIL_HEAD_DIM):
            qm = jnp.where(head_of_lane == h, q, jnp.zeros_like(q))
            s = _dot_nt(qm, kk) * (DIL_HEAD_DIM ** -0.5)
            s = jnp.where(valid, s, -jnp.inf)
            m = jnp.max(s, axis=-1, keepdims=True)
            p = jnp.exp(s - m)
            l = jnp.sum(p, axis=-1, keepdims=True)
            of = _dot(p.astype(BF16), vv)
            o_acc = jnp.where(head_of_lane == h, of / l, o_acc)
            lse_t = jnp.where(lane == h, m + jnp.log(l), lse_t)
        o_ref[0, rows, :] = o_acc.astype(BF16)
        lse_ref[0, rows, :] = lse_t


def _dil_attn(q3, k3, v3, gi, dilation):
    b, m, _ = q3.shape
    qr = DIL_QROWS
    sub = qr // DIL_BLOCK
    cur = pl.BlockSpec((1, qr, DIL_OUT_WIDTH), lambda i, r, n: (i, n, r))
    prev = pl.BlockSpec((1, DIL_BLOCK, DIL_OUT_WIDTH),
                        lambda i, r, n: (i, jnp.maximum(n * sub - 1, 0), r))
    return pl.pallas_call(
        _dil_attn_kernel,
        grid=(b, dilation, m // qr),
        in_specs=[cur, cur, prev, cur, prev],
        out_specs=[pl.BlockSpec((1, qr, DIL_OUT_WIDTH), lambda i, r, n: (i, n, r)),
                   pl.BlockSpec((1, qr, LANES), lambda i, r, n: (i, n, r))],
        out_shape=[jax.ShapeDtypeStruct((b, m, dilation * DIL_OUT_WIDTH), BF16),
                   jax.ShapeDtypeStruct((b, m, dilation * LANES), F32)],
        compiler_params=pltpu.CompilerParams(vmem_limit_bytes=VMEM_LIMIT_BYTES),
        name=f"dil_attn_g{gi}",
    )(q3, k3, k3, v3, v3)


def _mem_kv_kernel(mem_ref, nw_ref, w_ref, knw_ref, k_ref, v_ref):
    x = mem_ref[0]
    ms = jnp.mean(x * x, axis=-1, keepdims=True)
    u = (x * lax.rsqrt(ms + EPS) * nw_ref[...]).astype(BF16)
    width = MEM_HEADS * MEM_HEAD_PAD
    for h in range(MEM_HEADS):
        hs = slice(h * MEM_HEAD_PAD, (h + 1) * MEM_HEAD_PAD)
        kh = _dot(u, w_ref[:, hs])
        ss = jnp.sum(kh * kh, axis=-1, keepdims=True)
        k_ref[0, :, hs] = (kh * lax.rsqrt(ss / MEM_HEAD_DIM + EPS) * knw_ref[...]).astype(BF16)
        vs = slice(width + h * MEM_HEAD_PAD, width + (h + 1) * MEM_HEAD_PAD)
        v_ref[0, :, hs] = _dot(u, w_ref[:, vs]).astype(BF16)


def _mem_kv(mem, norm_w, w_kv, knw):
    b, n_mem, d = mem.shape
    width = MEM_HEADS * MEM_HEAD_PAD
    blk = pl.BlockSpec((1, n_mem, width), lambda i: (i, 0, 0))
    return pl.pallas_call(
        _mem_kv_kernel,
        grid=(b,),
        in_specs=[pl.BlockSpec((1, n_mem, d), lambda i: (i, 0, 0)), _resident((1, d)),
                  _resident(w_kv.shape), _resident((1, MEM_HEAD_PAD))],
        out_specs=[blk, blk],
        out_shape=[jax.ShapeDtypeStruct((b, n_mem, width), BF16)] * 2,
        compiler_params=pltpu.CompilerParams(vmem_limit_bytes=VMEM_LIMIT_BYTES),
        name="mem_kv",
    )(mem, norm_w, w_kv, knw)


MEM_QROWS = 512


def _mem_attn_kernel(q_ref, k_ref, v_ref, o_ref):
    for h in range(MEM_HEADS):
        hs = slice(h * MEM_HEAD_PAD, (h + 1) * MEM_HEAD_PAD)
        s = _dot_nt(q_ref[0, :, hs], k_ref[0, :, hs]) * (MEM_HEAD_DIM ** -0.5)
        m = jnp.max(s, axis=-1, keepdims=True)
        p = jnp.exp(s - m)
        l = jnp.sum(p, axis=-1, keepdims=True)
        o_ref[0, :, hs] = (_dot(p.astype(BF16), v_ref[0, :, hs]) / l).astype(BF16)


def _mem_attn(q, k, v):
    b, s, width = q.shape
    n_mem = k.shape[1]
    tm = MEM_QROWS
    kv = pl.BlockSpec((1, n_mem, width), lambda i, j: (i, 0, 0))
    return pl.pallas_call(
        _mem_attn_kernel,
        grid=(b, s // tm),
        in_specs=[pl.BlockSpec((1, tm, width), lambda i, j: (i, j, 0)), kv, kv],
        out_specs=pl.BlockSpec((1, tm, width), lambda i, j: (i, j, 0)),
        out_shape=jax.ShapeDtypeStruct((b, s, width), BF16),
        compiler_params=pltpu.CompilerParams(vmem_limit_bytes=VMEM_LIMIT_BYTES),
        name="mem_attn",
    )(q, k, v)


MERGE_ROWS = 512
MERGE_SLAB = 256


def _merge_kernel(x_ref, yssd_ref, o0_ref, o1_ref, o2_ref, l0_ref, l1_ref, l2_ref, ymem_ref,
                  gate_ref, wssd_ref, wdil_ref, wmem_ref, wout_ref, e4_ref, h_ref,
                  os0a_ref, os0b_ref, os1a_ref, os1b_ref, os2a_ref, os2b_ref,
                  ls0_ref, ls1_ref, ls2_ref):
    ostage = ((os0a_ref, os0b_ref), (os1a_ref, os1b_ref), (os2a_ref, os2b_ref))
    lstage = (ls0_ref, ls1_ref, ls2_ref)
    for gi, (o_ref, l_ref) in enumerate(((o0_ref, l0_ref), (o1_ref, l1_ref), (o2_ref, l2_ref))):
        dil = DIL_PAIRS[gi][1]
        n = MERGE_ROWS // dil
        for r in range(dil):
            tok = pl.ds(r, n, stride=dil) if dil > 1 else slice(None)
            for i in range(2):
                lo = r * DIL_OUT_WIDTH + i * LANES
                ostage[gi][i][tok, :] = o_ref[:, lo:lo + LANES].astype(F32)
            lstage[gi][tok, :] = l_ref[:, r * LANES:(r + 1) * LANES]
    for r0 in range(0, MERGE_ROWS, MERGE_SLAB):
        rows = slice(r0, r0 + MERGE_SLAB)
        l0, l1, l2 = ls0_ref[rows, :], ls1_ref[rows, :], ls2_ref[rows, :]
        lmax = jnp.maximum(jnp.maximum(l0, l1), l2)
        e0, e1, e2 = jnp.exp(l0 - lmax), jnp.exp(l1 - lmax), jnp.exp(l2 - lmax)
        inv = 1.0 / (e0 + e1 + e2)
        ydil = jnp.zeros((MERGE_SLAB, DIL_OUT_WIDTH), F32)
        for gi, e in enumerate((e0, e1, e2)):
            og = jnp.concatenate([ostage[gi][0][rows, :], ostage[gi][1][rows, :]], axis=1)
            ydil = ydil + _dot_f32(e * inv, e4_ref[...]) * og
        merged = (_sigmoid(gate_ref[rows, 0:D_MODEL].astype(F32))
                  * _dot(yssd_ref[rows, :], wssd_ref[...])
                  + _sigmoid(gate_ref[rows, D_MODEL:2 * D_MODEL].astype(F32))
                  * _dot(ydil.astype(BF16), wdil_ref[...])
                  + _sigmoid(gate_ref[rows, 2 * D_MODEL:3 * D_MODEL].astype(F32))
                  * _dot(ymem_ref[rows, :], wmem_ref[...]))
        h_ref[rows, :] = x_ref[rows, :] + _dot(merged.astype(BF16), wout_ref[...])


def _merge(x2d, yssd, o, lse, ymem, gate, wssd, wdil, wmem, wout, e4):
    m, d = x2d.shape
    tm = MERGE_ROWS
    row = lambda width, dil=1: pl.BlockSpec((tm // dil, width * dil), lambda i: (i, 0))
    dils = [dil for _, dil in DIL_PAIRS]
    ng = len(dils)
    return pl.pallas_call(
        _merge_kernel,
        grid=(m // tm,),
        in_specs=[row(d), row(d)] + [row(DIL_OUT_WIDTH, dil) for dil in dils]
                 + [row(LANES, dil) for dil in dils]
                 + [row(MEM_HEADS * MEM_HEAD_PAD), row(3 * d),
                    _resident(wssd.shape), _resident(wdil.shape), _resident(wmem.shape),
                    _resident(wout.shape), _resident(e4.shape)],
        out_specs=row(d),
        out_shape=jax.ShapeDtypeStruct((m, d), F32),
        scratch_shapes=[pltpu.VMEM((tm, LANES), F32)] * (3 * ng),
        compiler_params=pltpu.CompilerParams(vmem_limit_bytes=VMEM_LIMIT_BYTES),
        name="merge",
    )(x2d, yssd, o[0], o[1], o[2], lse[0], lse[1], lse[2], ymem, gate, wssd, wdil, wmem, wout, e4)


def _pad_heads(w, axis):
    shape = list(w.shape)
    shape[axis:axis + 1] = [MEM_HEADS, MEM_HEAD_DIM]
    w = w.reshape(shape)
    pad = [(0, 0)] * w.ndim
    pad[axis + 1] = (0, MEM_HEAD_PAD - MEM_HEAD_DIM)
    w = jnp.pad(w, pad)
    shape[axis:axis + 2] = [MEM_HEADS * MEM_HEAD_PAD]
    return w.reshape(shape)


def _mix_block(x, mem, positions, norm_mix_w, w_in, ssd_conv_w, ssd_conv_b, ssd_dt_bias,
               ssd_a_log, ssd_d, ssd_norm_w, dil_q_norm_w, dil_k_norm_w, mem_norm_w, w_mem_kv,
               mem_q_norm_w, mem_k_norm_w, w_ssd_br, w_dil_br, w_mem_br, w_out):
    b, s, d = x.shape
    m = b * s
    pad_lanes = lambda v: jnp.pad(v, (0, LANES - v.shape[0])).reshape(1, LANES)
    w_packed = jnp.concatenate([
        w_in[:, OFF_Z:OFF_DT],
        jnp.pad(w_in[:, OFF_DT:OFF_DQ], ((0, 0), (0, LANES - SSD_HEADS))),
        w_in[:, OFF_DQ:OFF_MQ],
        _pad_heads(w_in[:, OFF_MQ:OFF_GATE], 1),
        w_in[:, OFF_GATE:]], axis=1).astype(BF16)
    half = ROPE_DIMS // 2
    inv = jnp.exp(-math.log(ROPE_THETA) * (2.0 / ROPE_DIMS) * jnp.arange(half, dtype=F32))
    invp = jnp.tile(inv, LANES // half).reshape(1, LANES)
    pos_b = jnp.broadcast_to(positions.astype(F32).reshape(m, 1), (m, LANES))
    tile2 = lambda v: jnp.tile(v, 2).reshape(1, LANES)
    mqnw = jnp.pad(mem_q_norm_w, (0, MEM_HEAD_PAD - MEM_HEAD_DIM)).reshape(1, MEM_HEAD_PAD)
    mknw = jnp.pad(mem_k_norm_w, (0, MEM_HEAD_PAD - MEM_HEAD_DIM)).reshape(1, MEM_HEAD_PAD)
    z, xbc, dt_raw, q0, q1, q2, k0, k1, k2, v0, v1, v2, mq, gate = _in_proj(
        x.reshape(m, d), pos_b, norm_mix_w.reshape(1, d), w_packed,
        tile2(dil_q_norm_w), tile2(dil_k_norm_w), mqnw, invp)

    expand = jnp.repeat(jnp.eye(LANES, SSD_HEADS, dtype=F32), SSD_HEAD_DIM, axis=1)
    y_ssd = _ssd(xbc.reshape(b, s, -1), z.reshape(b, s, -1), dt_raw.reshape(b, s, -1),
                 ssd_conv_w, ssd_conv_b.reshape(1, -1), pad_lanes(ssd_dt_bias), pad_lanes(ssd_a_log),
                 jnp.repeat(ssd_d, SSD_HEAD_DIM).reshape(1, d), ssd_norm_w.reshape(1, d), expand)

    outs, lses = [], []
    for gi, (window, dilation) in enumerate(DIL_PAIRS):
        assert window // dilation == DIL_BLOCK
        per_batch = lambda t: t.reshape(b, s // dilation, -1)
        o, lse = _dil_attn(per_batch((q0, q1, q2)[gi]), per_batch((k0, k1, k2)[gi]),
                           per_batch((v0, v1, v2)[gi]), gi, dilation)
        outs.append(o.reshape(m // dilation, -1))
        lses.append(lse.reshape(m // dilation, -1))

    w_kv = jnp.concatenate([_pad_heads(w_mem_kv[:, :MEM_WIDTH], 1),
                            _pad_heads(w_mem_kv[:, MEM_WIDTH:], 1)], axis=1).astype(BF16)
    mk, mv = _mem_kv(mem, mem_norm_w.reshape(1, d), w_kv, mknw)
    y_mem = _mem_attn(mq.reshape(b, s, -1), mk, mv).reshape(m, -1)

    e4 = jnp.repeat(jnp.eye(LANES, DIL_OUT_WIDTH // DIL_HEAD_DIM, dtype=F32), DIL_HEAD_DIM, axis=1)
    return _merge(x.reshape(m, d), y_ssd.reshape(m, d), outs, lses, y_mem, gate,
                  w_ssd_br.astype(BF16), w_dil_br.astype(BF16),
                  _pad_heads(w_mem_br, 0).astype(BF16), w_out.astype(BF16), e4)


PEER_TOKENS = 512
PEER_A_PER_STEP = 4


def _oddeven_merge(lo, hi, r):
    step = r * 2
    if step < hi - lo:
        yield from _oddeven_merge(lo, hi, step)
        yield from _oddeven_merge(lo + r, hi, step)
        yield from [(i, i + r) for i in range(lo + r, hi - r, step)]
    else:
        yield (lo, lo + r)


def _oddeven_merge_sort(lo, hi):
    if hi - lo >= 1:
        mid = lo + (hi - lo) // 2
        yield from _oddeven_merge_sort(lo, mid)
        yield from _oddeven_merge_sort(mid + 1, hi)
        yield from _oddeven_merge(lo, hi, 1)


_SORT16 = tuple(_oddeven_merge_sort(0, PEER_TOPK - 1))


def _cmpx(a, i, j):
    hi = jnp.maximum(a[i], a[j])
    lo = jnp.minimum(a[i], a[j])
    a[i], a[j] = hi, lo


def _sort16_desc(a):
    for i, j in _SORT16:
        _cmpx(a, i, j)
    return a


def _bitonic16_desc(a):
    for d in (8, 4, 2, 1):
        for i in range(PEER_TOPK):
            if not i & d:
                _cmpx(a, i, i + d)
    return a


def _merge_top16(a, b):
    c = [jnp.maximum(a[k], b[PEER_TOPK - 1 - k]) for k in range(PEER_TOPK)]
    return _bitonic16_desc(c)


def _merge_sublanes(a):
    for sh in (4, 2, 1):
        b = [pltpu.roll(x, sh, axis=0) for x in a]
        a = _merge_top16(a, b)
    return a


def _top16_rows(s):
    rows = [s[8 * i:8 * i + 8, :] for i in range(PEER_N_KEYS // 8)]
    return _merge_sublanes(_sort16_desc(rows))


def _peer_route_kernel(h_ref, nw_ref, wqt_ref, keys_ref,
                       hnt_ref, cnt_ref, f1_ref, rank_ref, e2_ref):
    t = h_ref.shape[0]
    x = h_ref[...]
    ms = jnp.mean(x * x, axis=-1, keepdims=True)
    hn = x * lax.rsqrt(ms + EPS) * nw_ref[...]
    hnt = hn.T.astype(BF16)
    hnt_ref[...] = hnt
    sub = lax.broadcasted_iota(jnp.int32, (8, t), 0)

    def distribute(vs):
        out = vs[7]
        for r in range(6, -1, -1):
            out = jnp.where(sub == r, vs[r], out)
        return out

    for h in range(PEER_HEADS):
        qt = jnp.dot(wqt_ref[h * PEER_QUERY_DIM:(h + 1) * PEER_QUERY_DIM, :], hnt,
                     preferred_element_type=F32).astype(BF16)
        half = PEER_QUERY_DIM // 2
        s1 = jnp.dot(keys_ref[h, 0], qt[:half], preferred_element_type=F32)
        s2 = jnp.dot(keys_ref[h, 1], qt[half:], preferred_element_type=F32)
        v1 = _top16_rows(s1)
        v2 = _top16_rows(s2)
        a_lo = distribute(v1[:8])
        a_hi = distribute(v1[8:])
        lo = [a_lo + v2[j] for j in range(PEER_TOPK)]
        hi = [a_hi + v2[j] for j in range(PEER_TOPK)]
        tau = _merge_sublanes(_merge_top16(lo, hi))[PEER_TOPK - 1]
        e_lo = jnp.exp(a_lo - v1[0])
        e_hi = jnp.exp(a_hi - v1[0])
        zp = jnp.zeros((8, t), F32)
        for j in range(PEER_TOPK):
            e2j = jnp.exp(v2[j] - v2[0])
            zp = zp + jnp.where(lo[j] >= tau, e_lo * e2j, 0.0)
            zp = zp + jnp.where(hi[j] >= tau, e_hi * e2j, 0.0)
        z = jnp.sum(zp, axis=0, keepdims=True)
        tau_r = tau[0:1, :]
        cnt = jnp.zeros(s1.shape, F32)
        rank = jnp.zeros(s2.shape, F32)
        for j in range(PEER_TOPK):
            v2j = v2[j][0:1, :]
            cnt = jnp.where(s1 + v2j >= tau_r, j + 1.0, cnt)
            rank = jnp.where(v2j > s2, j + 1.0, rank)
        cnt_ref[h] = cnt
        f1_ref[h] = jnp.exp(s1 - v1[0][0:1, :]) / z
        rank_ref[h] = rank.astype(BF16)
        e2_ref[h] = jnp.exp(s2 - v2[0][0:1, :]).astype(BF16)


def _peer_route(h2d, norm_w, wqt, keys):
    n, d = h2d.shape
    t = PEER_TOKENS
    route = lambda dt: jax.ShapeDtypeStruct((PEER_HEADS, PEER_N_KEYS, n), dt)
    route_spec = pl.BlockSpec((PEER_HEADS, PEER_N_KEYS, t), lambda i: (0, 0, i))
    return pl.pallas_call(
        _peer_route_kernel,
        grid=(n // t,),
        in_specs=[
            pl.BlockSpec((t, d), lambda i: (i, 0)),
            pl.BlockSpec((1, d), lambda i: (0, 0)),
            pl.BlockSpec(wqt.shape, lambda i: (0, 0)),
            pl.BlockSpec(keys.shape, lambda i: (0, 0, 0, 0)),
        ],
        out_specs=[pl.BlockSpec((d, t), lambda i: (0, i)),
                   route_spec, route_spec, route_spec, route_spec],
        out_shape=[jax.ShapeDtypeStruct((d, n), BF16), route(F32), route(F32), route(BF16), route(BF16)],
        compiler_params=pltpu.CompilerParams(vmem_limit_bytes=VMEM_LIMIT_BYTES),
        name="peer_route",
    )(h2d, norm_w.reshape(1, d), wqt, keys)


BF16_ROWS = 16


def _peer_expert_kernel(h_ref, hnt_ref, cnt_ref, f1_ref, rank_ref, e2_ref,
                        down_ref, upt_ref, o_ref, acc_ref, coef_ref, w_ref, cntb_ref, f1b_ref):
    step = pl.program_id(1)
    n_blocks = pl.num_programs(1) - 1
    t = hnt_ref.shape[1]
    cur = step % 2

    @pl.when(step == 0)
    def _():
        acc_ref[...] = jnp.zeros_like(acc_ref)
        coef_ref[1] = jnp.zeros_like(coef_ref[1])

    block = jnp.minimum(step, n_blocks - 1)
    for q in range(PEER_A_PER_STEP):
        a = block * PEER_A_PER_STEP + q
        for h in range(PEER_HEADS):
            i = q * PEER_HEADS + h
            cntb_ref[i] = jnp.broadcast_to(cnt_ref[h, pl.ds(a, 1), :], (BF16_ROWS, t)).astype(BF16)
            f1b_ref[i] = jnp.broadcast_to(f1_ref[h, pl.ds(a, 1), :], (BF16_ROWS, t)).astype(BF16)
    for c0 in range(0, t, 2 * LANES):
        cols = slice(c0, c0 + 2 * LANES)
        for r0 in range(0, PEER_N_KEYS, BF16_ROWS):
            rows = slice(r0, r0 + BF16_ROWS)
            ranks = [rank_ref[h, rows, cols] for h in range(PEER_HEADS)]
            e2s = [e2_ref[h, rows, cols] for h in range(PEER_HEADS)]
            for q in range(PEER_A_PER_STEP):
                w = None
                for h in range(PEER_HEADS):
                    i = q * PEER_HEADS + h
                    wh = jnp.where(ranks[h] < cntb_ref[i, :, cols], e2s[h],
                                   jnp.zeros_like(e2s[h])) * f1b_ref[i, :, cols]
                    w = wh if w is None else w + wh
                er = slice(q * PEER_N_KEYS + r0, q * PEER_N_KEYS + r0 + BF16_ROWS)
                w_ref[er, cols] = w

    acc_ref[...] += jnp.dot(upt_ref[...], coef_ref[1 - cur], preferred_element_type=F32)
    s = jnp.dot(down_ref[...], hnt_ref[...], preferred_element_type=F32)
    act = (0.5 * s * (1.0 + lax.erf(s * (2.0 ** -0.5)))).astype(BF16)
    coef_ref[cur] = w_ref[...] * act

    @pl.when(step == n_blocks)
    def _():
        o_ref[...] = h_ref[...] + acc_ref[...].T


def _peer_expert(h2d, hnt, cnt, f1, rank, e2, down_bf, upt_bf):
    n, d = h2d.shape
    t = PEER_TOKENS
    eb = PEER_A_PER_STEP * PEER_N_KEYS
    pairs = PEER_A_PER_STEP * PEER_HEADS
    route_spec = pl.BlockSpec((PEER_HEADS, PEER_N_KEYS, t), lambda i, j: (0, 0, i))
    n_blocks = PEER_N_KEYS // PEER_A_PER_STEP
    return pl.pallas_call(
        _peer_expert_kernel,
        grid=(n // t, n_blocks + 1),
        in_specs=[
            pl.BlockSpec((t, d), lambda i, j: (i, 0)),
            pl.BlockSpec((d, t), lambda i, j: (0, i)),
            route_spec, route_spec, route_spec, route_spec,
            pl.BlockSpec((eb, d), lambda i, j: (jnp.minimum(j, n_blocks - 1), 0)),
            pl.BlockSpec((d, eb), lambda i, j: (0, jnp.maximum(j - 1, 0))),
        ],
        out_specs=pl.BlockSpec((t, d), lambda i, j: (i, 0)),
        out_shape=jax.ShapeDtypeStruct((n, d), F32),
        scratch_shapes=[pltpu.VMEM((d, t), F32), pltpu.VMEM((2, eb, t), BF16), pltpu.VMEM((eb, t), BF16),
                        pltpu.VMEM((pairs, BF16_ROWS, t), BF16), pltpu.VMEM((pairs, BF16_ROWS, t), BF16)],
        compiler_params=pltpu.CompilerParams(
            dimension_semantics=("arbitrary", "arbitrary"),
            vmem_limit_bytes=VMEM_LIMIT_BYTES),
        name="peer_expert",
    )(h2d, hnt, cnt, f1, rank, e2, down_bf, upt_bf)


def _peer_block(h2d, norm_w, w_query, sub_keys, expert_down, expert_up):
    wqt = w_query.T.astype(BF16)
    hnt, cnt, f1, rank, e2 = _peer_route(h2d, norm_w, wqt, sub_keys.astype(BF16))
    return _peer_expert(h2d, hnt, cnt, f1, rank, e2,
                        expert_down.astype(BF16), expert_up.T.astype(BF16))


def kernel(x, mem, positions, norm_mix_w, w_in, ssd_conv_w, ssd_conv_b, ssd_dt_bias,
           ssd_a_log, ssd_d, ssd_norm_w, dil_q_norm_w, dil_k_norm_w, mem_norm_w, w_mem_kv,
           mem_q_norm_w, mem_k_norm_w, w_ssd_br, w_dil_br, w_mem_br, w_out, norm_ffn_w,
           peer_w_query, peer_sub_keys, peer_down, peer_up):
    b, s, d = x.shape
    assert norm_mix_w.shape[0] == 1, "single-layer block"
    h = _mix_block(x, mem, positions, norm_mix_w[0], w_in[0], ssd_conv_w[0], ssd_conv_b[0],
                   ssd_dt_bias[0], ssd_a_log[0], ssd_d[0], ssd_norm_w[0], dil_q_norm_w[0],
                   dil_k_norm_w[0], mem_norm_w[0], w_mem_kv[0], mem_q_norm_w[0], mem_k_norm_w[0],
                   w_ssd_br[0], w_dil_br[0], w_mem_br[0], w_out[0])
    h = _peer_block(h, norm_ffn_w[0], peer_w_query[0], peer_sub_keys[0], peer_down[0], peer_up[0])
    return h.reshape(b, s, d)
```

```python
import math

import jax
import jax.numpy as jnp
from jax import lax
from jax.experimental import pallas as pl
from jax.experimental.pallas import tpu as pltpu

D_MODEL = 1024
EPS = 1e-6
SSD_HEAD_DIM = 64
SSD_HEADS = 16
SSD_GROUPS = 4
SSD_HEADS_PER_GROUP = 4
SSD_STATE = 128
SSD_CONV = 4
SSD_CHUNK = 128
SSD_CONV_DIM = 2048
DIL_PAIRS = ((128, 1), (512, 4), (2048, 16))
DIL_HEAD_DIM = 64
DIL_WIDTH = 768
DIL_OUT_WIDTH = 256
MEM_HEADS = 4
MEM_HEAD_DIM = 192
MEM_HEAD_PAD = 256
MEM_WIDTH = 768
ROPE_THETA = 500000.0
ROPE_DIMS = 16
PEER_HEADS = 8
PEER_N_KEYS = 128
PEER_QUERY_DIM = 256
PEER_TOPK = 16
OFF_Z = 0
OFF_XBC = 1024
OFF_DT = 3072
OFF_DQ = 3088
OFF_DK = OFF_DQ + DIL_WIDTH
OFF_DV = OFF_DK + DIL_WIDTH
OFF_MQ = OFF_DV + DIL_WIDTH
OFF_GATE = OFF_MQ + MEM_WIDTH
F32 = jnp.float32
BF16 = jnp.bfloat16
HIGHEST = lax.Precision.HIGHEST
LANES = 128
VMEM_LIMIT_BYTES = 56 * 1024 * 1024

P_Z = 0
P_XBC = 1024
P_DT = 3072
P_Q = 3200
P_K = P_Q + DIL_WIDTH
P_V = P_K + DIL_WIDTH
P_MQ = P_V + DIL_WIDTH
P_GATE = P_MQ + MEM_HEADS * MEM_HEAD_PAD
P_WIDTH = P_GATE + 3 * D_MODEL


def _resident(shape):
    nd = len(shape)
    return pl.BlockSpec(shape, lambda *_: (0,) * nd, pipeline_mode=pl.Buffered(1))


def _sigmoid(x):
    return 1.0 / (1.0 + jnp.exp(-x))


def _dot(a, b):
    return jnp.dot(a, b, preferred_element_type=F32)


def _dot_nt(a, b):
    return lax.dot_general(a, b, (((1,), (1,)), ((), ())), preferred_element_type=F32)


def _dot_f32(a, b):
    return jnp.dot(a, b, precision=HIGHEST, preferred_element_type=F32)


IN_PROJ_ROWS = 512
PLAIN_CHUNK = 256


def _in_proj_kernel(x_ref, pos_ref, nw_ref, w_ref, qnw_ref, knw_ref, mqnw_ref, invp_ref,
                    z_ref, xbc_ref, dt_ref, q0_ref, q1_ref, q2_ref, k0_ref, k1_ref, k2_ref,
                    v0_ref, v1_ref, v2_ref, mq_ref, gate_ref, u_ref, stage_a_ref, stage_b_ref):
    x = x_ref[...]
    ms = jnp.mean(x * x, axis=-1, keepdims=True)
    u_ref[...] = (x * lax.rsqrt(ms + EPS) * nw_ref[...]).astype(BF16)

    def seg(off, width):
        return _dot(u_ref[...], w_ref[:, off:off + width])

    for j in range(0, 1024, PLAIN_CHUNK):
        z_ref[:, j:j + PLAIN_CHUNK] = seg(P_Z + j, PLAIN_CHUNK).astype(BF16)
    for j in range(0, SSD_CONV_DIM, PLAIN_CHUNK):
        xbc_ref[:, j:j + PLAIN_CHUNK] = seg(P_XBC + j, PLAIN_CHUNK).astype(BF16)
    dt_ref[...] = seg(P_DT, LANES)
    rows = x.shape[0]

    stages = (stage_a_ref, stage_b_ref)

    def put_group(gi, o_ref):
        dil = DIL_PAIRS[gi][1]
        for i, st in enumerate(stages):
            for r in range(dil):
                tok = pl.ds(r, rows // dil, stride=dil) if dil > 1 else slice(None)
                lo = r * DIL_OUT_WIDTH + i * LANES
                o_ref[:, lo:lo + LANES] = st[tok, :].astype(BF16)

    for gi, o_ref in enumerate((v0_ref, v1_ref, v2_ref)):
        vt = seg(P_V + gi * DIL_OUT_WIDTH, DIL_OUT_WIDTH)
        for i, st in enumerate(stages):
            st[...] = vt[:, i * LANES:(i + 1) * LANES]
        put_group(gi, o_ref)
    for j in range(0, 3 * D_MODEL, PLAIN_CHUNK):
        gate_ref[:, j:j + PLAIN_CHUNK] = seg(P_GATE + j, PLAIN_CHUNK).astype(BF16)

    ang = pos_ref[...] * invp_ref[...]
    cos = jnp.cos(ang)
    sin = jnp.sin(ang)
    lane = lax.broadcasted_iota(jnp.int32, (rows, LANES), 1)
    l64 = lane & (DIL_HEAD_DIM - 1)
    first = l64 < ROPE_DIMS // 2
    rotated = l64 < ROPE_DIMS
    cm = jnp.where(rotated, cos, 1.0)
    sm = jnp.where(rotated, jnp.where(first, -sin, sin), 0.0)
    left = lane < DIL_HEAD_DIM

    def qk_norm_rope(off, w_row_ref, o_refs):
        for gi, o_ref in enumerate(o_refs):
            t2 = seg(off + gi * DIL_OUT_WIDTH, DIL_OUT_WIDTH)
            for i in range(2):
                t = t2[:, i * LANES:(i + 1) * LANES]
                sq = t * t
                ssl = jnp.sum(jnp.where(left, sq, 0.0), axis=-1, keepdims=True)
                ssr = jnp.sum(jnp.where(left, 0.0, sq), axis=-1, keepdims=True)
                r = jnp.where(left, lax.rsqrt(ssl / DIL_HEAD_DIM + EPS),
                              lax.rsqrt(ssr / DIL_HEAD_DIM + EPS))
                tn = t * r * w_row_ref[...]
                rot = jnp.where(first, pltpu.roll(tn, LANES - ROPE_DIMS // 2, axis=1),
                                pltpu.roll(tn, ROPE_DIMS // 2, axis=1))
                stages[i][...] = tn * cm + rot * sm
            put_group(gi, o_ref)

    qk_norm_rope(P_Q, qnw_ref, (q0_ref, q1_ref, q2_ref))
    qk_norm_rope(P_K, knw_ref, (k0_ref, k1_ref, k2_ref))

    for h in range(MEM_HEADS):
        t2 = seg(P_MQ + h * MEM_HEAD_PAD, MEM_HEAD_PAD)
        ss = jnp.sum(t2 * t2, axis=-1, keepdims=True)
        mq_ref[:, h * MEM_HEAD_PAD:(h + 1) * MEM_HEAD_PAD] = (
            t2 * lax.rsqrt(ss / MEM_HEAD_DIM + EPS) * mqnw_ref[...]).astype(BF16)


def _in_proj(x2d, pos_b, norm_w, w_packed, qnw, knw, mqnw, invp):
    m, d = x2d.shape
    tm = IN_PROJ_ROWS
    row = lambda width, dil=1: pl.BlockSpec((tm // dil, width * dil), lambda i: (i, 0))
    dil_outs = [(DIL_OUT_WIDTH, BF16, dil) for _, dil in DIL_PAIRS]
    outs = ([(1024, BF16, 1), (SSD_CONV_DIM, BF16, 1), (LANES, F32, 1)] + dil_outs * 3
            + [(MEM_HEADS * MEM_HEAD_PAD, BF16, 1), (3 * D_MODEL, BF16, 1)])
    return pl.pallas_call(
        _in_proj_kernel,
        grid=(m // tm,),
        in_specs=[row(d), row(LANES), _resident((1, d)), _resident(w_packed.shape),
                  _resident((1, LANES)), _resident((1, LANES)), _resident((1, MEM_HEAD_PAD)),
                  _resident((1, LANES))],
        out_specs=[row(w, dil) for w, _, dil in outs],
        out_shape=[jax.ShapeDtypeStruct((m // dil, w * dil), dt) for w, dt, dil in outs],
        scratch_shapes=[pltpu.VMEM((tm, d), BF16), pltpu.VMEM((tm, LANES), F32),
                        pltpu.VMEM((tm, LANES), F32)],
        compiler_params=pltpu.CompilerParams(vmem_limit_bytes=VMEM_LIMIT_BYTES),
        name="in_proj",
    )(x2d, pos_b, norm_w, w_packed, qnw, knw, mqnw, invp)


def _ssd_kernel(xbc_ref, z_ref, dt_ref, cw_ref, cb_ref, dtb_ref, alog_ref, dfull_ref, nw_ref,
                e_ref, y_ref, tail_ref, state_ref):
    c = pl.program_id(1)
    L = SSD_CHUNK

    @pl.when(c == 0)
    def _():
        tail_ref[...] = jnp.zeros_like(tail_ref)
        state_ref[...] = jnp.zeros_like(state_ref)

    cur = xbc_ref[0].astype(F32)
    xp = jnp.concatenate([tail_ref[...], cur], axis=0)
    conv = cb_ref[...] + cw_ref[SSD_CONV - 1:SSD_CONV, :] * cur
    for k in range(SSD_CONV - 1):
        lo = 8 - (SSD_CONV - 1) + k
        conv = conv + cw_ref[k:k + 1, :] * xp[lo:lo + L]
    tail_ref[...] = cur[L - 8:L]
    xa = conv * _sigmoid(conv)
    xs = xa[:, :D_MODEL]
    gn = SSD_GROUPS * SSD_STATE
    bm = xa[:, D_MODEL:D_MODEL + gn].astype(BF16)
    cm = xa[:, D_MODEL + gn:].astype(BF16)

    pre = dt_ref[0] + dtb_ref[...]
    dt = jnp.maximum(pre, 0.0) + jnp.log(1.0 + jnp.exp(-jnp.abs(pre)))
    da = dt * (-jnp.exp(alog_ref[...]))
    ri = lax.broadcasted_iota(jnp.int32, (L, L), 0)
    ci = lax.broadcasted_iota(jnp.int32, (L, L), 1)
    tril = ri >= ci
    acum = _dot_f32(tril.astype(F32), da)
    acum_t = acum.T
    def expand(v):
        hi = v.astype(BF16)
        lo = (v - hi.astype(F32)).astype(BF16)
        return _dot(hi, e_ref[...]) + _dot(lo, e_ref[...])

    dt_full = expand(dt)
    ea_full = expand(jnp.exp(acum))
    dec_full = expand(jnp.exp(acum[L - 1:L, :] - acum))
    xdt = xs * dt_full
    xdt_b = xdt.astype(BF16)
    xdd_b = (xdt * dec_full).astype(BF16)
    gw = SSD_HEADS_PER_GROUP * SSD_HEAD_DIM
    head_of_lane = lax.broadcasted_iota(jnp.int32, (L, gw), 1) // SSD_HEAD_DIM
    ys = []
    for g in range(SSD_GROUPS):
        bg = bm[:, g * SSD_STATE:(g + 1) * SSD_STATE]
        cg = cm[:, g * SSD_STATE:(g + 1) * SSD_STATE]
        gs = slice(g * gw, (g + 1) * gw)
        cb = _dot_nt(cg, bg)
        st_old = state_ref[g]
        yg = _dot(cg, st_old.astype(BF16)) * ea_full[:, gs]
        ms = []
        for r in range(SSD_HEADS_PER_GROUP):
            h = g * SSD_HEADS_PER_GROUP + r
            seg = acum[:, h:h + 1] - acum_t[h:h + 1, :]
            lmat = jnp.exp(jnp.where(tril, seg, -jnp.inf))
            ms.append((cb * lmat).astype(BF16))
        yd = _dot(jnp.concatenate(ms, axis=0), xdt_b[:, gs])
        for r in range(SSD_HEADS_PER_GROUP):
            yg = yg + jnp.where(head_of_lane == r, yd[r * L:(r + 1) * L], 0.0)
        ys.append(yg)
        s_new = lax.dot_general(bg, xdd_b[:, gs], (((0,), (0,)), ((), ())),
                                preferred_element_type=F32)
        state_ref[g] = st_old * ea_full[L - 1:L, gs] + s_new
    y = jnp.concatenate(ys, axis=1) + dfull_ref[...] * xs
    zf = z_ref[0].astype(F32)
    y = y * (zf * _sigmoid(zf))
    outs = []
    for g in range(SSD_GROUPS):
        yg = y[:, g * gw:(g + 1) * gw]
        ss = jnp.sum(yg * yg, axis=-1, keepdims=True)
        outs.append(yg * lax.rsqrt(ss / gw + EPS))
    y_ref[0] = (jnp.concatenate(outs, axis=1) * nw_ref[...]).astype(BF16)


def _ssd(xbc, z, dt_raw, conv_w, conv_b, dt_bias, a_log, d_full, norm_w, expand):
    b, s, _ = xbc.shape
    L = SSD_CHUNK
    blk = lambda width: pl.BlockSpec((1, L, width), lambda i, c: (i, c, 0))
    return pl.pallas_call(
        _ssd_kernel,
        grid=(b, s // L),
        in_specs=[blk(SSD_CONV_DIM), blk(D_MODEL), blk(LANES),
                  _resident(conv_w.shape), _resident(conv_b.shape), _resident(dt_bias.shape),
                  _resident(a_log.shape), _resident(d_full.shape), _resident(norm_w.shape),
                  _resident(expand.shape)],
        out_specs=blk(D_MODEL),
        out_shape=jax.ShapeDtypeStruct((b, s, D_MODEL), BF16),
        scratch_shapes=[pltpu.VMEM((8, SSD_CONV_DIM), F32),
                        pltpu.VMEM((SSD_GROUPS, SSD_STATE, SSD_HEADS_PER_GROUP * SSD_HEAD_DIM), F32)],
        compiler_params=pltpu.CompilerParams(
            dimension_semantics=("arbitrary", "arbitrary"), vmem_limit_bytes=VMEM_LIMIT_BYTES),
        name="ssd",
    )(xbc, z, dt_raw, conv_w, conv_b, dt_bias, a_log, d_full, norm_w, expand)


DIL_BLOCK = 128
DIL_QROWS = 512


def _dil_attn_kernel(q_ref, kc_ref, kp_ref, vc_ref, vp_ref, o_ref, lse_ref):
    n = pl.program_id(2)
    blk = DIL_BLOCK
    w = DIL_OUT_WIDTH
    nh = w // DIL_HEAD_DIM
    head_of_lane = lax.broadcasted_iota(jnp.int32, (blk, w), 1) // DIL_HEAD_DIM
    lane = lax.broadcasted_iota(jnp.int32, (blk, LANES), 1)
    qi = lax.broadcasted_iota(jnp.int32, (nh * blk, 2 * blk), 0) % blk
    ki = lax.broadcasted_iota(jnp.int32, (nh * blk, 2 * blk), 1)
    band = (ki >= qi) & (ki <= qi + blk)
    first_valid = band & (ki + jnp.minimum(n, 1) * blk >= blk)
    for sb in range(DIL_QROWS // blk):
        rows = slice(sb * blk, (sb + 1) * blk)
        q = q_ref[0, rows, :]
        if sb == 0:
            kprev, vprev, valid = kp_ref[0], vp_ref[0], first_valid
        else:
            prev = slice((sb - 1) * blk, sb * blk)
            kprev, vprev, valid = kc_ref[0, prev, :], vc_ref[0, prev, :], band
        kk = jnp.concatenate([kprev, kc_ref[0, rows, :]], axis=0)
        vv = jnp.concatenate([vprev, vc_ref[0, rows, :]], axis=0)
        qs = jnp.concatenate([jnp.where(head_of_lane == h, q, jnp.zeros_like(q))
                              for h in range(nh)], axis=0)
        s = _dot_nt(qs, kk) * (DIL_HEAD_DIM ** -0.5)
        s = jnp.where(valid, s, -jnp.inf)
        m = jnp.max(s, axis=-1, keepdims=True)
        p = jnp.exp(s - m)
        l = jnp.sum(p, axis=-1, keepdims=True)
        on = _dot(p.astype(BF16), vv) / l
        lse = m + jnp.log(l)
        o_acc = jnp.zeros((blk, w), F32)
        lse_t = jnp.zeros((blk, LANES), F32)
        for h in range(nh):
            hr = slice(h * blk, (h + 1) * blk)
            o_acc = jnp.where(head_of_lane == h, on[hr], o_acc)
            lse_t = jnp.where(lane == h, lse[hr], lse_t)
        o_ref[0, rows, :] = o_acc.astype(BF16)
        lse_ref[0, rows, :] = lse_t


def _dil_attn(q3, k3, v3, gi, dilation):
    b, m, _ = q3.shape
    qr = DIL_QROWS
    sub = qr // DIL_BLOCK
    cur = pl.BlockSpec((1, qr, DIL_OUT_WIDTH), lambda i, r, n: (i, n, r))
    prev = pl.BlockSpec((1, DIL_BLOCK, DIL_OUT_WIDTH),
                        lambda i, r, n: (i, jnp.maximum(n * sub - 1, 0), r))
    return pl.pallas_call(
        _dil_attn_kernel,
        grid=(b, dilation, m // qr),
        in_specs=[cur, cur, prev, cur, prev],
        out_specs=[pl.BlockSpec((1, qr, DIL_OUT_WIDTH), lambda i, r, n: (i, n, r)),
                   pl.BlockSpec((1, qr, LANES), lambda i, r, n: (i, n, r))],
        out_shape=[jax.ShapeDtypeStruct((b, m, dilation * DIL_OUT_WIDTH), BF16),
                   jax.ShapeDtypeStruct((b, m, dilation * LANES), F32)],
        compiler_params=pltpu.CompilerParams(vmem_limit_bytes=VMEM_LIMIT_BYTES),
        name=f"dil_attn_g{gi}",
    )(q3, k3, k3, v3, v3)


def _mem_kv_kernel(mem_ref, nw_ref, w_ref, knw_ref, k_ref, v_ref):
    x = mem_ref[0]
    ms = jnp.mean(x * x, axis=-1, keepdims=True)
    u = (x * lax.rsqrt(ms + EPS) * nw_ref[...]).astype(BF16)
    width = MEM_HEADS * MEM_HEAD_PAD
    for h in range(MEM_HEADS):
        hs = slice(h * MEM_HEAD_PAD, (h + 1) * MEM_HEAD_PAD)
        kh = _dot(u, w_ref[:, hs])
        ss = jnp.sum(kh * kh, axis=-1, keepdims=True)
        k_ref[0, :, hs] = (kh * lax.rsqrt(ss / MEM_HEAD_DIM + EPS) * knw_ref[...]).astype(BF16)
        vs = slice(width + h * MEM_HEAD_PAD, width + (h + 1) * MEM_HEAD_PAD)
        v_ref[0, :, hs] = _dot(u, w_ref[:, vs]).astype(BF16)


def _mem_kv(mem, norm_w, w_kv, knw):
    b, n_mem, d = mem.shape
    width = MEM_HEADS * MEM_HEAD_PAD
    blk = pl.BlockSpec((1, n_mem, width), lambda i: (i, 0, 0))
    return pl.pallas_call(
        _mem_kv_kernel,
        grid=(b,),
        in_specs=[pl.BlockSpec((1, n_mem, d), lambda i: (i, 0, 0)), _resident((1, d)),
                  _resident(w_kv.shape), _resident((1, MEM_HEAD_PAD))],
        out_specs=[blk, blk],
        out_shape=[jax.ShapeDtypeStruct((b, n_mem, width), BF16)] * 2,
        compiler_params=pltpu.CompilerParams(vmem_limit_bytes=VMEM_LIMIT_BYTES),
        name="mem_kv",
    )(mem, norm_w, w_kv, knw)


MEM_QROWS = 512


def _mem_attn_kernel(q_ref, k_ref, v_ref, o_ref):
    for h in range(MEM_HEADS):
        hs = slice(h * MEM_HEAD_PAD, (h + 1) * MEM_HEAD_PAD)
        s = _dot_nt(q_ref[0, :, hs], k_ref[0, :, hs]) * (MEM_HEAD_DIM ** -0.5)
        m = jnp.max(s, axis=-1, keepdims=True)
        p = jnp.exp(s - m)
        l = jnp.sum(p, axis=-1, keepdims=True)
        o_ref[0, :, hs] = (_dot(p.astype(BF16), v_ref[0, :, hs]) / l).astype(BF16)


def _mem_attn(q, k, v):
    b, s, width = q.shape
    n_mem = k.shape[1]
    tm = MEM_QROWS
    kv = pl.BlockSpec((1, n_mem, width), lambda i, j: (i, 0, 0))
    return pl.pallas_call(
        _mem_attn_kernel,
        grid=(b, s // tm),
        in_specs=[pl.BlockSpec((1, tm, width), lambda i, j: (i, j, 0)), kv, kv],
        out_specs=pl.BlockSpec((1, tm, width), lambda i, j: (i, j, 0)),
        out_shape=jax.ShapeDtypeStruct((b, s, width), BF16),
        compiler_params=pltpu.CompilerParams(vmem_limit_bytes=VMEM_LIMIT_BYTES),
        name="mem_attn",
    )(q, k, v)


MERGE_ROWS = 512
MERGE_SLAB = 256


def _merge_kernel(x_ref, yssd_ref, o0_ref, o1_ref, o2_ref, l0_ref, l1_ref, l2_ref, ymem_ref,
                  gate_ref, wssd_ref, wdil_ref, wmem_ref, wout_ref, e4_ref, h_ref,
                  os0a_ref, os0b_ref, os1a_ref, os1b_ref, os2a_ref, os2b_ref,
                  ls0_ref, ls1_ref, ls2_ref):
    ostage = ((os0a_ref, os0b_ref), (os1a_ref, os1b_ref), (os2a_ref, os2b_ref))
    lstage = (ls0_ref, ls1_ref, ls2_ref)
    for gi, (o_ref, l_ref) in enumerate(((o0_ref, l0_ref), (o1_ref, l1_ref), (o2_ref, l2_ref))):
        dil = DIL_PAIRS[gi][1]
        n = MERGE_ROWS // dil
        for r in range(dil):
            tok = pl.ds(r, n, stride=dil) if dil > 1 else slice(None)
            for i in range(2):
                lo = r * DIL_OUT_WIDTH + i * LANES
                ostage[gi][i][tok, :] = o_ref[:, lo:lo + LANES].astype(F32)
            lstage[gi][tok, :] = l_ref[:, r * LANES:(r + 1) * LANES]
    for r0 in range(0, MERGE_ROWS, MERGE_SLAB):
        rows = slice(r0, r0 + MERGE_SLAB)
        l0, l1, l2 = ls0_ref[rows, :], ls1_ref[rows, :], ls2_ref[rows, :]
        lmax = jnp.maximum(jnp.maximum(l0, l1), l2)
        e0, e1, e2 = jnp.exp(l0 - lmax), jnp.exp(l1 - lmax), jnp.exp(l2 - lmax)
        inv = 1.0 / (e0 + e1 + e2)
        ydil = jnp.zeros((MERGE_SLAB, DIL_OUT_WIDTH), F32)
        for gi, e in enumerate((e0, e1, e2)):
            og = jnp.concatenate([ostage[gi][0][rows, :], ostage[gi][1][rows, :]], axis=1)
            ydil = ydil + _dot_f32(e * inv, e4_ref[...]) * og
        merged = (_sigmoid(gate_ref[rows, 0:D_MODEL].astype(F32))
                  * _dot(yssd_ref[rows, :], wssd_ref[...])
                  + _sigmoid(gate_ref[rows, D_MODEL:2 * D_MODEL].astype(F32))
                  * _dot(ydil.astype(BF16), wdil_ref[...])
                  + _sigmoid(gate_ref[rows, 2 * D_MODEL:3 * D_MODEL].astype(F32))
                  * _dot(ymem_ref[rows, :], wmem_ref[...]))
        h_ref[rows, :] = x_ref[rows, :] + _dot(merged.astype(BF16), wout_ref[...])


def _merge(x2d, yssd, o, lse, ymem, gate, wssd, wdil, wmem, wout, e4):
    m, d = x2d.shape
    tm = MERGE_ROWS
    row = lambda width, dil=1: pl.BlockSpec((tm // dil, width * dil), lambda i: (i, 0))
    dils = [dil for _, dil in DIL_PAIRS]
    ng = len(dils)
    return pl.pallas_call(
        _merge_kernel,
        grid=(m // tm,),
        in_specs=[row(d), row(d)] + [row(DIL_OUT_WIDTH, dil) for dil in dils]
                 + [row(LANES, dil) for dil in dils]
                 + [row(MEM_HEADS * MEM_HEAD_PAD), row(3 * d),
                    _resident(wssd.shape), _resident(wdil.shape), _resident(wmem.shape),
                    _resident(wout.shape), _resident(e4.shape)],
        out_specs=row(d),
        out_shape=jax.ShapeDtypeStruct((m, d), F32),
        scratch_shapes=[pltpu.VMEM((tm, LANES), F32)] * (3 * ng),
        compiler_params=pltpu.CompilerParams(vmem_limit_bytes=VMEM_LIMIT_BYTES),
        name="merge",
    )(x2d, yssd, o[0], o[1], o[2], lse[0], lse[1], lse[2], ymem, gate, wssd, wdil, wmem, wout, e4)


def _pad_heads(w, axis):
    shape = list(w.shape)
    shape[axis:axis + 1] = [MEM_HEADS, MEM_HEAD_DIM]
    w = w.reshape(shape)
    pad = [(0, 0)] * w.ndim
    pad[axis + 1] = (0, MEM_HEAD_PAD - MEM_HEAD_DIM)
    w = jnp.pad(w, pad)
    shape[axis:axis + 2] = [MEM_HEADS * MEM_HEAD_PAD]
    return w.reshape(shape)


def _mix_block(x, mem, positions, norm_mix_w, w_in, ssd_conv_w, ssd_conv_b, ssd_dt_bias,
               ssd_a_log, ssd_d, ssd_norm_w, dil_q_norm_w, dil_k_norm_w, mem_norm_w, w_mem_kv,
               mem_q_norm_w, mem_k_norm_w, w_ssd_br, w_dil_br, w_mem_br, w_out):
    b, s, d = x.shape
    m = b * s
    pad_lanes = lambda v: jnp.pad(v, (0, LANES - v.shape[0])).reshape(1, LANES)
    w_packed = jnp.concatenate([
        w_in[:, OFF_Z:OFF_DT],
        jnp.pad(w_in[:, OFF_DT:OFF_DQ], ((0, 0), (0, LANES - SSD_HEADS))),
        w_in[:, OFF_DQ:OFF_MQ],
        _pad_heads(w_in[:, OFF_MQ:OFF_GATE], 1),
        w_in[:, OFF_GATE:]], axis=1).astype(BF16)
    half = ROPE_DIMS // 2
    inv = jnp.exp(-math.log(ROPE_THETA) * (2.0 / ROPE_DIMS) * jnp.arange(half, dtype=F32))
    invp = jnp.tile(inv, LANES // half).reshape(1, LANES)
    pos_b = jnp.broadcast_to(positions.astype(F32).reshape(m, 1), (m, LANES))
    tile2 = lambda v: jnp.tile(v, 2).reshape(1, LANES)
    mqnw = jnp.pad(mem_q_norm_w, (0, MEM_HEAD_PAD - MEM_HEAD_DIM)).reshape(1, MEM_HEAD_PAD)
    mknw = jnp.pad(mem_k_norm_w, (0, MEM_HEAD_PAD - MEM_HEAD_DIM)).reshape(1, MEM_HEAD_PAD)
    z, xbc, dt_raw, q0, q1, q2, k0, k1, k2, v0, v1, v2, mq, gate = _in_proj(
        x.reshape(m, d), pos_b, norm_mix_w.reshape(1, d), w_packed,
        tile2(dil_q_norm_w), tile2(dil_k_norm_w), mqnw, invp)

    expand = jnp.repeat(jnp.eye(LANES, SSD_HEADS, dtype=BF16), SSD_HEAD_DIM, axis=1)
    y_ssd = _ssd(xbc.reshape(b, s, -1), z.reshape(b, s, -1), dt_raw.reshape(b, s, -1),
                 ssd_conv_w, ssd_conv_b.reshape(1, -1), pad_lanes(ssd_dt_bias), pad_lanes(ssd_a_log),
                 jnp.repeat(ssd_d, SSD_HEAD_DIM).reshape(1, d), ssd_norm_w.reshape(1, d), expand)

    outs, lses = [], []
    for gi, (window, dilation) in enumerate(DIL_PAIRS):
        assert window // dilation == DIL_BLOCK
        per_batch = lambda t: t.reshape(b, s // dilation, -1)
        o, lse = _dil_attn(per_batch((q0, q1, q2)[gi]), per_batch((k0, k1, k2)[gi]),
                           per_batch((v0, v1, v2)[gi]), gi, dilation)
        outs.append(o.reshape(m // dilation, -1))
        lses.append(lse.reshape(m // dilation, -1))

    w_kv = jnp.concatenate([_pad_heads(w_mem_kv[:, :MEM_WIDTH], 1),
                            _pad_heads(w_mem_kv[:, MEM_WIDTH:], 1)], axis=1).astype(BF16)
    mk, mv = _mem_kv(mem, mem_norm_w.reshape(1, d), w_kv, mknw)
    y_mem = _mem_attn(mq.reshape(b, s, -1), mk, mv).reshape(m, -1)

    e4 = jnp.repeat(jnp.eye(LANES, DIL_OUT_WIDTH // DIL_HEAD_DIM, dtype=F32), DIL_HEAD_DIM, axis=1)
    return _merge(x.reshape(m, d), y_ssd.reshape(m, d), outs, lses, y_mem, gate,
                  w_ssd_br.astype(BF16), w_dil_br.astype(BF16),
                  _pad_heads(w_mem_br, 0).astype(BF16), w_out.astype(BF16), e4)


PEER_TOKENS = 512
PEER_A_PER_STEP = 8
ROUTE_LANES = 128


def _oddeven_merge(lo, hi, r):
    step = r * 2
    if step < hi - lo:
        yield from _oddeven_merge(lo, hi, step)
        yield from _oddeven_merge(lo + r, hi, step)
        yield from [(i, i + r) for i in range(lo + r, hi - r, step)]
    else:
        yield (lo, lo + r)


def _oddeven_merge_sort(lo, hi):
    if hi - lo >= 1:
        mid = lo + (hi - lo) // 2
        yield from _oddeven_merge_sort(lo, mid)
        yield from _oddeven_merge_sort(mid + 1, hi)
        yield from _oddeven_merge(lo, hi, 1)


_SORT16 = tuple(_oddeven_merge_sort(0, PEER_TOPK - 1))


def _cmpx(a, i, j):
    hi = jnp.maximum(a[i], a[j])
    lo = jnp.minimum(a[i], a[j])
    a[i], a[j] = hi, lo


def _sort16_desc(a):
    for i, j in _SORT16:
        _cmpx(a, i, j)
    return a


def _bitonic16_desc(a):
    for d in (8, 4, 2, 1):
        for i in range(PEER_TOPK):
            if not i & d:
                _cmpx(a, i, i + d)
    return a


def _merge_top16(a, b):
    c = [jnp.maximum(a[k], b[PEER_TOPK - 1 - k]) for k in range(PEER_TOPK)]
    return _bitonic16_desc(c)


def _merge_sublanes(a):
    for sh in (4, 2, 1):
        b = [pltpu.roll(x, sh, axis=0) for x in a]
        a = _merge_top16(a, b)
    return a


def _top16_rows(s):
    rows = [s[8 * i:8 * i + 8, :] for i in range(PEER_N_KEYS // 8)]
    return _merge_sublanes(_sort16_desc(rows))


def _peer_route_kernel(h_ref, nw_ref, wqt_ref, keys_ref,
                       hnt_ref, cnt_ref, f1_ref, rank_ref, e2_ref, s1_ref, s2_ref):
    t = h_ref.shape[0]
    x = h_ref[...]
    ms = jnp.mean(x * x, axis=-1, keepdims=True)
    hn = x * lax.rsqrt(ms + EPS) * nw_ref[...]
    hnt = hn.T.astype(BF16)
    hnt_ref[...] = hnt
    ch = ROUTE_LANES
    sub = lax.broadcasted_iota(jnp.int32, (8, ch), 0)

    def distribute(vs):
        out = vs[7]
        for r in range(6, -1, -1):
            out = jnp.where(sub == r, vs[r], out)
        return out

    for h in range(PEER_HEADS):
        qt = jnp.dot(wqt_ref[h * PEER_QUERY_DIM:(h + 1) * PEER_QUERY_DIM, :], hnt,
                     preferred_element_type=F32).astype(BF16)
        half = PEER_QUERY_DIM // 2
        s1_ref[...] = jnp.dot(keys_ref[h, 0], qt[:half], preferred_element_type=F32)
        s2_ref[...] = jnp.dot(keys_ref[h, 1], qt[half:], preferred_element_type=F32)
        for c0 in range(0, t, ch):
            cols = slice(c0, c0 + ch)
            s1 = s1_ref[:, cols]
            s2 = s2_ref[:, cols]
            v1 = _top16_rows(s1)
            v2 = _top16_rows(s2)
            a_lo = distribute(v1[:8])
            a_hi = distribute(v1[8:])
            lo = [a_lo + v2[j] for j in range(PEER_TOPK)]
            hi = [a_hi + v2[j] for j in range(PEER_TOPK)]
            tau = _merge_sublanes(_merge_top16(lo, hi))[PEER_TOPK - 1]
            e_lo = jnp.exp(a_lo - v1[0])
            e_hi = jnp.exp(a_hi - v1[0])
            zp = jnp.zeros((8, ch), F32)
            for j in range(PEER_TOPK):
                e2j = jnp.exp(v2[j] - v2[0])
                zp = zp + jnp.where(lo[j] >= tau, e_lo * e2j, 0.0)
                zp = zp + jnp.where(hi[j] >= tau, e_hi * e2j, 0.0)
            z = jnp.sum(zp, axis=0, keepdims=True)
            tau_r = tau[0:1, :]
            cnt = jnp.zeros(s1.shape, F32)
            rank = jnp.zeros(s2.shape, F32)
            for j in range(PEER_TOPK):
                v2j = v2[j][0:1, :]
                cnt = jnp.where(s1 + v2j >= tau_r, j + 1.0, cnt)
                rank = jnp.where(v2j > s2, j + 1.0, rank)
            cnt_ref[h, :, cols] = cnt
            f1_ref[h, :, cols] = jnp.exp(s1 - v1[0][0:1, :]) / z
            rank_ref[h, :, cols] = rank.astype(BF16)
            e2_ref[h, :, cols] = jnp.exp(s2 - v2[0][0:1, :]).astype(BF16)


def _peer_route(h2d, norm_w, wqt, keys):
    n, d = h2d.shape
    t = PEER_TOKENS
    route = lambda dt: jax.ShapeDtypeStruct((PEER_HEADS, PEER_N_KEYS, n), dt)
    route_spec = pl.BlockSpec((PEER_HEADS, PEER_N_KEYS, t), lambda i: (0, 0, i))
    return pl.pallas_call(
        _peer_route_kernel,
        grid=(n // t,),
        in_specs=[
            pl.BlockSpec((t, d), lambda i: (i, 0)),
            pl.BlockSpec((1, d), lambda i: (0, 0)),
            pl.BlockSpec(wqt.shape, lambda i: (0, 0)),
            pl.BlockSpec(keys.shape, lambda i: (0, 0, 0, 0)),
        ],
        out_specs=[pl.BlockSpec((d, t), lambda i: (0, i)),
                   route_spec, route_spec, route_spec, route_spec],
        out_shape=[jax.ShapeDtypeStruct((d, n), BF16), route(F32), route(F32), route(BF16), route(BF16)],
        scratch_shapes=[pltpu.VMEM((PEER_N_KEYS, t), F32), pltpu.VMEM((PEER_N_KEYS, t), F32)],
        compiler_params=pltpu.CompilerParams(vmem_limit_bytes=VMEM_LIMIT_BYTES),
        name="peer_route",
    )(h2d, norm_w.reshape(1, d), wqt, keys)


BF16_ROWS = 16


def _peer_expert_kernel(h_ref, hnt_ref, cnt_ref, f1_ref, rank_ref, e2_ref,
                        down_ref, upt_ref, o_ref, acc_ref, coef_ref, w_ref, cntb_ref, f1b_ref):
    step = pl.program_id(1)
    n_blocks = pl.num_programs(1) - 1
    t = hnt_ref.shape[1]
    cur = step % 2

    @pl.when(step == 0)
    def _():
        acc_ref[...] = jnp.zeros_like(acc_ref)
        coef_ref[1] = jnp.zeros_like(coef_ref[1])

    block = jnp.minimum(step, n_blocks - 1)
    for q in range(PEER_A_PER_STEP):
        a = block * PEER_A_PER_STEP + q
        for h in range(PEER_HEADS):
            i = q * PEER_HEADS + h
            cntb_ref[i] = jnp.broadcast_to(cnt_ref[h, pl.ds(a, 1), :], (BF16_ROWS, t)).astype(BF16)
            f1b_ref[i] = jnp.broadcast_to(f1_ref[h, pl.ds(a, 1), :], (BF16_ROWS, t)).astype(BF16)
    for c0 in range(0, t, 2 * LANES):
        cols = slice(c0, c0 + 2 * LANES)
        for r0 in range(0, PEER_N_KEYS, BF16_ROWS):
            rows = slice(r0, r0 + BF16_ROWS)
            ranks = [rank_ref[h, rows, cols] for h in range(PEER_HEADS)]
            e2s = [e2_ref[h, rows, cols] for h in range(PEER_HEADS)]
            for q in range(PEER_A_PER_STEP):
                w = None
                for h in range(PEER_HEADS):
                    i = q * PEER_HEADS + h
                    wh = jnp.where(ranks[h] < cntb_ref[i, :, cols], e2s[h],
                                   jnp.zeros_like(e2s[h])) * f1b_ref[i, :, cols]
                    w = wh if w is None else w + wh
                er = slice(q * PEER_N_KEYS + r0, q * PEER_N_KEYS + r0 + BF16_ROWS)
                w_ref[er, cols] = w

    acc_ref[...] += jnp.dot(upt_ref[...], coef_ref[1 - cur], preferred_element_type=F32)
    s = jnp.dot(down_ref[...], hnt_ref[...], preferred_element_type=F32)
    sb = s.astype(BF16)
    act = (0.5 * sb) * (1.0 + lax.erf(sb * (2.0 ** -0.5)))
    coef_ref[cur] = w_ref[...] * act

    @pl.when(step == n_blocks)
    def _():
        o_ref[...] = h_ref[...] + acc_ref[...].T


def _peer_expert(h2d, hnt, cnt, f1, rank, e2, down_bf, upt_bf):
    n, d = h2d.shape
    t = PEER_TOKENS
    eb = PEER_A_PER_STEP * PEER_N_KEYS
    pairs = PEER_A_PER_STEP * PEER_HEADS
    route_spec = pl.BlockSpec((PEER_HEADS, PEER_N_KEYS, t), lambda i, j: (0, 0, i))
    n_blocks = PEER_N_KEYS // PEER_A_PER_STEP
    return pl.pallas_call(
        _peer_expert_kernel,
        grid=(n // t, n_blocks + 1),
        in_specs=[
            pl.BlockSpec((t, d), lambda i, j: (i, 0)),
            pl.BlockSpec((d, t), lambda i, j: (0, i)),
            route_spec, route_spec, route_spec, route_spec,
            pl.BlockSpec((eb, d), lambda i, j: (jnp.minimum(j, n_blocks - 1), 0)),
            pl.BlockSpec((d, eb), lambda i, j: (0, jnp.maximum(j - 1, 0))),
        ],
        out_specs=pl.BlockSpec((t, d), lambda i, j: (i, 0)),
        out_shape=jax.ShapeDtypeStruct((n, d), F32),
        scratch_shapes=[pltpu.VMEM((d, t), F32), pltpu.VMEM((2, eb, t), BF16), pltpu.VMEM((eb, t), BF16),
                        pltpu.VMEM((pairs, BF16_ROWS, t), BF16), pltpu.VMEM((pairs, BF16_ROWS, t), BF16)],
        compiler_params=pltpu.CompilerParams(
            dimension_semantics=("arbitrary", "arbitrary"),
            vmem_limit_bytes=VMEM_LIMIT_BYTES),
        name="peer_expert",
    )(h2d, hnt, cnt, f1, rank, e2, down_bf, upt_bf)


def _peer_block(h2d, norm_w, w_query, sub_keys, expert_down, expert_up):
    wqt = w_query.T.astype(BF16)
    hnt, cnt, f1, rank, e2 = _peer_route(h2d, norm_w, wqt, sub_keys.astype(BF16))
    return _peer_expert(h2d, hnt, cnt, f1, rank, e2,
                        expert_down.astype(BF16), expert_up.T.astype(BF16))


def kernel(x, mem, positions, norm_mix_w, w_in, ssd_conv_w, ssd_conv_b, ssd_dt_bias,
           ssd_a_log, ssd_d, ssd_norm_w, dil_q_norm_w, dil_k_norm_w, mem_norm_w, w_mem_kv,
           mem_q_norm_w, mem_k_norm_w, w_ssd_br, w_dil_br, w_mem_br, w_out, norm_ffn_w,
           peer_w_query, peer_sub_keys, peer_down, peer_up):
    b, s, d = x.shape
    assert norm_mix_w.shape[0] == 1, "single-layer block"
    h = _mix_block(x, mem, positions, norm_mix_w[0], w_in[0], ssd_conv_w[0], ssd_conv_b[0],
                   ssd_dt_bias[0], ssd_a_log[0], ssd_d[0], ssd_norm_w[0], dil_q_norm_w[0],
                   dil_k_norm_w[0], mem_norm_w[0], w_mem_kv[0], mem_q_norm_w[0], mem_k_norm_w[0],
                   w_ssd_br[0], w_dil_br[0], w_mem_br[0], w_out[0])
    h = _peer_block(h, norm_ffn_w[0], peer_w_query[0], peer_sub_keys[0], peer_down[0], peer_up[0])
    return h.reshape(b, s, d)
```

```python
import math

import jax
import jax.numpy as jnp
from jax import lax
from jax.experimental import pallas as pl
from jax.experimental.pallas import tpu as pltpu

D_MODEL = 1024
EPS = 1e-6
SSD_HEAD_DIM = 64
SSD_HEADS = 16
SSD_GROUPS = 4
SSD_HEADS_PER_GROUP = 4
SSD_STATE = 128
SSD_CONV = 4
SSD_CHUNK = 128
SSD_CONV_DIM = 2048
DIL_PAIRS = ((128, 1), (512, 4), (2048, 16))
DIL_HEAD_DIM = 64
DIL_WIDTH = 768
DIL_OUT_WIDTH = 256
MEM_HEADS = 4
MEM_HEAD_DIM = 192
MEM_HEAD_PAD = 256
MEM_WIDTH = 768
ROPE_THETA = 500000.0
ROPE_DIMS = 16
PEER_HEADS = 8
PEER_N_KEYS = 128
PEER_QUERY_DIM = 256
PEER_TOPK = 16
OFF_Z = 0
OFF_XBC = 1024
OFF_DT = 3072
OFF_DQ = 3088
OFF_DK = OFF_DQ + DIL_WIDTH
OFF_DV = OFF_DK + DIL_WIDTH
OFF_MQ = OFF_DV + DIL_WIDTH
OFF_GATE = OFF_MQ + MEM_WIDTH
F32 = jnp.float32
BF16 = jnp.bfloat16
HIGHEST = lax.Precision.HIGHEST
LANES = 128
VMEM_LIMIT_BYTES = 56 * 1024 * 1024

P_Z = 0
P_XBC = 1024
P_DT = 3072
P_Q = 3200
P_K = P_Q + DIL_WIDTH
P_V = P_K + DIL_WIDTH
P_MQ = P_V + DIL_WIDTH
P_GATE = P_MQ + MEM_HEADS * MEM_HEAD_PAD
P_WIDTH = P_GATE + 3 * D_MODEL


def _resident(shape):
    nd = len(shape)
    return pl.BlockSpec(shape, lambda *_: (0,) * nd, pipeline_mode=pl.Buffered(1))


def _sigmoid(x):
    return 1.0 / (1.0 + jnp.exp(-x))


def _dot(a, b):
    return jnp.dot(a, b, preferred_element_type=F32)


def _dot_nt(a, b):
    return lax.dot_general(a, b, (((1,), (1,)), ((), ())), preferred_element_type=F32)


def _dot_f32(a, b):
    return jnp.dot(a, b, precision=HIGHEST, preferred_element_type=F32)


IN_PROJ_ROWS = 512
PLAIN_CHUNK = 256


def _in_proj_kernel(x_ref, pos_ref, nw_ref, w_ref, qnw_ref, knw_ref, mqnw_ref, invp_ref,
                    z_ref, xbc_ref, dt_ref, q0_ref, q1_ref, q2_ref, k0_ref, k1_ref, k2_ref,
                    v0_ref, v1_ref, v2_ref, mq_ref, gate_ref, u_ref, stage_a_ref, stage_b_ref):
    x = x_ref[...]
    ms = jnp.mean(x * x, axis=-1, keepdims=True)
    u_ref[...] = (x * lax.rsqrt(ms + EPS) * nw_ref[...]).astype(BF16)

    def seg(off, width):
        return _dot(u_ref[...], w_ref[:, off:off + width])

    for j in range(0, 1024, PLAIN_CHUNK):
        z_ref[:, j:j + PLAIN_CHUNK] = seg(P_Z + j, PLAIN_CHUNK).astype(BF16)
    for j in range(0, SSD_CONV_DIM, PLAIN_CHUNK):
        xbc_ref[:, j:j + PLAIN_CHUNK] = seg(P_XBC + j, PLAIN_CHUNK).astype(BF16)
    dt_ref[...] = seg(P_DT, LANES)
    rows = x.shape[0]

    stages = (stage_a_ref, stage_b_ref)

    def put_group(gi, o_ref):
        dil = DIL_PAIRS[gi][1]
        for i, st in enumerate(stages):
            for r in range(dil):
                tok = pl.ds(r, rows // dil, stride=dil) if dil > 1 else slice(None)
                lo = r * DIL_OUT_WIDTH + i * LANES
                o_ref[:, lo:lo + LANES] = st[tok, :].astype(BF16)

    for gi, o_ref in enumerate((v0_ref, v1_ref, v2_ref)):
        vt = seg(P_V + gi * DIL_OUT_WIDTH, DIL_OUT_WIDTH)
        for i, st in enumerate(stages):
            st[...] = vt[:, i * LANES:(i + 1) * LANES]
        put_group(gi, o_ref)
    for j in range(0, 3 * D_MODEL, PLAIN_CHUNK):
        gate_ref[:, j:j + PLAIN_CHUNK] = seg(P_GATE + j, PLAIN_CHUNK).astype(BF16)

    ang = pos_ref[...] * invp_ref[...]
    cos = jnp.cos(ang)
    sin = jnp.sin(ang)
    lane = lax.broadcasted_iota(jnp.int32, (rows, LANES), 1)
    l64 = lane & (DIL_HEAD_DIM - 1)
    first = l64 < ROPE_DIMS // 2
    rotated = l64 < ROPE_DIMS
    cm = jnp.where(rotated, cos, 1.0)
    sm = jnp.where(rotated, jnp.where(first, -sin, sin), 0.0)
    left = lane < DIL_HEAD_DIM

    def qk_norm_rope(off, w_row_ref, o_refs):
        for gi, o_ref in enumerate(o_refs):
            t2 = seg(off + gi * DIL_OUT_WIDTH, DIL_OUT_WIDTH)
            for i in range(2):
                t = t2[:, i * LANES:(i + 1) * LANES]
                sq = t * t
                ssl = jnp.sum(jnp.where(left, sq, 0.0), axis=-1, keepdims=True)
                ssr = jnp.sum(jnp.where(left, 0.0, sq), axis=-1, keepdims=True)
                r = jnp.where(left, lax.rsqrt(ssl / DIL_HEAD_DIM + EPS),
                              lax.rsqrt(ssr / DIL_HEAD_DIM + EPS))
                tn = t * r * w_row_ref[...]
                rot = jnp.where(first, pltpu.roll(tn, LANES - ROPE_DIMS // 2, axis=1),
                                pltpu.roll(tn, ROPE_DIMS // 2, axis=1))
                stages[i][...] = tn * cm + rot * sm
            put_group(gi, o_ref)

    qk_norm_rope(P_Q, qnw_ref, (q0_ref, q1_ref, q2_ref))
    qk_norm_rope(P_K, knw_ref, (k0_ref, k1_ref, k2_ref))

    for h in range(MEM_HEADS):
        t2 = seg(P_MQ + h * MEM_HEAD_PAD, MEM_HEAD_PAD)
        ss = jnp.sum(t2 * t2, axis=-1, keepdims=True)
        mq_ref[:, h * MEM_HEAD_PAD:(h + 1) * MEM_HEAD_PAD] = (
            t2 * lax.rsqrt(ss / MEM_HEAD_DIM + EPS) * mqnw_ref[...]).astype(BF16)


def _in_proj(x2d, pos_b, norm_w, w_packed, qnw, knw, mqnw, invp):
    m, d = x2d.shape
    tm = IN_PROJ_ROWS
    row = lambda width, dil=1: pl.BlockSpec((tm // dil, width * dil), lambda i: (i, 0))
    dil_outs = [(DIL_OUT_WIDTH, BF16, dil) for _, dil in DIL_PAIRS]
    outs = ([(1024, BF16, 1), (SSD_CONV_DIM, BF16, 1), (LANES, F32, 1)] + dil_outs * 3
            + [(MEM_HEADS * MEM_HEAD_PAD, BF16, 1), (3 * D_MODEL, BF16, 1)])
    return pl.pallas_call(
        _in_proj_kernel,
        grid=(m // tm,),
        in_specs=[row(d), row(LANES), _resident((1, d)), _resident(w_packed.shape),
                  _resident((1, LANES)), _resident((1, LANES)), _resident((1, MEM_HEAD_PAD)),
                  _resident((1, LANES))],
        out_specs=[row(w, dil) for w, _, dil in outs],
        out_shape=[jax.ShapeDtypeStruct((m // dil, w * dil), dt) for w, dt, dil in outs],
        scratch_shapes=[pltpu.VMEM((tm, d), BF16), pltpu.VMEM((tm, LANES), F32),
                        pltpu.VMEM((tm, LANES), F32)],
        compiler_params=pltpu.CompilerParams(vmem_limit_bytes=VMEM_LIMIT_BYTES),
        name="in_proj",
    )(x2d, pos_b, norm_w, w_packed, qnw, knw, mqnw, invp)


def _ssd_kernel(xbc_ref, z_ref, dt_ref, cw_ref, cb_ref, dtb_ref, alog_ref, dfull_ref, nw_ref,
                e_ref, y_ref, tail_ref, state_ref):
    c = pl.program_id(1)
    L = SSD_CHUNK

    @pl.when(c == 0)
    def _():
        tail_ref[...] = jnp.zeros_like(tail_ref)
        state_ref[...] = jnp.zeros_like(state_ref)

    cur = xbc_ref[0].astype(F32)
    xp = jnp.concatenate([tail_ref[...], cur], axis=0)
    conv = cb_ref[...] + cw_ref[SSD_CONV - 1:SSD_CONV, :] * cur
    for k in range(SSD_CONV - 1):
        lo = 8 - (SSD_CONV - 1) + k
        conv = conv + cw_ref[k:k + 1, :] * xp[lo:lo + L]
    tail_ref[...] = cur[L - 8:L]
    xa = conv * _sigmoid(conv)
    xs = xa[:, :D_MODEL]
    gn = SSD_GROUPS * SSD_STATE
    bm = xa[:, D_MODEL:D_MODEL + gn].astype(BF16)
    cm = xa[:, D_MODEL + gn:].astype(BF16)

    pre = dt_ref[0] + dtb_ref[...]
    dt = jnp.maximum(pre, 0.0) + jnp.log(1.0 + jnp.exp(-jnp.abs(pre)))
    da = dt * (-jnp.exp(alog_ref[...]))
    ri = lax.broadcasted_iota(jnp.int32, (L, L), 0)
    ci = lax.broadcasted_iota(jnp.int32, (L, L), 1)
    tril = ri >= ci
    acum = _dot_f32(tril.astype(F32), da)
    acum_t = acum.T
    def expand(v):
        hi = v.astype(BF16)
        lo = (v - hi.astype(F32)).astype(BF16)
        return _dot(hi, e_ref[...]) + _dot(lo, e_ref[...])

    dt_full = expand(dt)
    ea_full = expand(jnp.exp(acum))
    dec_full = expand(jnp.exp(acum[L - 1:L, :] - acum))
    xdt = xs * dt_full
    xdt_b = xdt.astype(BF16)
    xdd_b = (xdt * dec_full).astype(BF16)
    gw = SSD_HEADS_PER_GROUP * SSD_HEAD_DIM
    head_of_lane = lax.broadcasted_iota(jnp.int32, (L, gw), 1) // SSD_HEAD_DIM
    ys = []
    for g in range(SSD_GROUPS):
        bg = bm[:, g * SSD_STATE:(g + 1) * SSD_STATE]
        cg = cm[:, g * SSD_STATE:(g + 1) * SSD_STATE]
        gs = slice(g * gw, (g + 1) * gw)
        cb = _dot_nt(cg, bg)
        st_old = state_ref[g]
        yg = _dot(cg, st_old.astype(BF16)) * ea_full[:, gs]
        ms = []
        for r in range(SSD_HEADS_PER_GROUP):
            h = g * SSD_HEADS_PER_GROUP + r
            seg = acum[:, h:h + 1] - acum_t[h:h + 1, :]
            lmat = jnp.exp(jnp.where(tril, seg, -jnp.inf))
            ms.append((cb * lmat).astype(BF16))
        yd = _dot(jnp.concatenate(ms, axis=0), xdt_b[:, gs])
        for r in range(SSD_HEADS_PER_GROUP):
            yg = yg + jnp.where(head_of_lane == r, yd[r * L:(r + 1) * L], 0.0)
        ys.append(yg)
        s_new = lax.dot_general(bg, xdd_b[:, gs], (((0,), (0,)), ((), ())),
                                preferred_element_type=F32)
        state_ref[g] = st_old * ea_full[L - 1:L, gs] + s_new
    y = jnp.concatenate(ys, axis=1) + dfull_ref[...] * xs
    zf = z_ref[0].astype(F32)
    y = y * (zf * _sigmoid(zf))
    outs = []
    for g in range(SSD_GROUPS):
        yg = y[:, g * gw:(g + 1) * gw]
        ss = jnp.sum(yg * yg, axis=-1, keepdims=True)
        outs.append(yg * lax.rsqrt(ss / gw + EPS))
    y_ref[0] = (jnp.concatenate(outs, axis=1) * nw_ref[...]).astype(BF16)


def _ssd(xbc, z, dt_raw, conv_w, conv_b, dt_bias, a_log, d_full, norm_w, expand):
    b, s, _ = xbc.shape
    L = SSD_CHUNK
    blk = lambda width: pl.BlockSpec((1, L, width), lambda i, c: (i, c, 0))
    return pl.pallas_call(
        _ssd_kernel,
        grid=(b, s // L),
        in_specs=[blk(SSD_CONV_DIM), blk(D_MODEL), blk(LANES),
                  _resident(conv_w.shape), _resident(conv_b.shape), _resident(dt_bias.shape),
                  _resident(a_log.shape), _resident(d_full.shape), _resident(norm_w.shape),
                  _resident(expand.shape)],
        out_specs=blk(D_MODEL),
        out_shape=jax.ShapeDtypeStruct((b, s, D_MODEL), BF16),
        scratch_shapes=[pltpu.VMEM((8, SSD_CONV_DIM), F32),
                        pltpu.VMEM((SSD_GROUPS, SSD_STATE, SSD_HEADS_PER_GROUP * SSD_HEAD_DIM), F32)],
        compiler_params=pltpu.CompilerParams(
            dimension_semantics=("arbitrary", "arbitrary"), vmem_limit_bytes=VMEM_LIMIT_BYTES),
        name="ssd",
    )(xbc, z, dt_raw, conv_w, conv_b, dt_bias, a_log, d_full, norm_w, expand)


DIL_BLOCK = 128
DIL_QROWS = 512


def _dil_attn_kernel(q_ref, kc_ref, kp_ref, vc_ref, vp_ref, o_ref, lse_ref):
    n = pl.program_id(2)
    blk = DIL_BLOCK
    w = DIL_OUT_WIDTH
    nh = w // DIL_HEAD_DIM
    head_of_lane = lax.broadcasted_iota(jnp.int32, (blk, w), 1) // DIL_HEAD_DIM
    lane = lax.broadcasted_iota(jnp.int32, (blk, LANES), 1)
    qi = lax.broadcasted_iota(jnp.int32, (nh * blk, 2 * blk), 0) % blk
    ki = lax.broadcasted_iota(jnp.int32, (nh * blk, 2 * blk), 1)
    band = (ki >= qi) & (ki <= qi + blk)
    first_valid = band & (ki + jnp.minimum(n, 1) * blk >= blk)
    for sb in range(DIL_QROWS // blk):
        rows = slice(sb * blk, (sb + 1) * blk)
        q = q_ref[0, rows, :]
        if sb == 0:
            kprev, vprev, valid = kp_ref[0], vp_ref[0], first_valid
        else:
            prev = slice((sb - 1) * blk, sb * blk)
            kprev, vprev, valid = kc_ref[0, prev, :], vc_ref[0, prev, :], band
        kk = jnp.concatenate([kprev, kc_ref[0, rows, :]], axis=0)
        vv = jnp.concatenate([vprev, vc_ref[0, rows, :]], axis=0)
        qs = jnp.concatenate([jnp.where(head_of_lane == h, q, jnp.zeros_like(q))
                              for h in range(nh)], axis=0)
        s = _dot_nt(qs, kk) * (DIL_HEAD_DIM ** -0.5)
        s = jnp.where(valid, s, -jnp.inf)
        m = jnp.max(s, axis=-1, keepdims=True)
        p = jnp.exp(s - m)
        l = jnp.sum(p, axis=-1, keepdims=True)
        on = _dot(p.astype(BF16), vv) / l
        lse = m + jnp.log(l)
        o_acc = jnp.zeros((blk, w), F32)
        lse_t = jnp.zeros((blk, LANES), F32)
        for h in range(nh):
            hr = slice(h * blk, (h + 1) * blk)
            o_acc = jnp.where(head_of_lane == h, on[hr], o_acc)
            lse_t = jnp.where(lane == h, lse[hr], lse_t)
        o_ref[0, rows, :] = o_acc.astype(BF16)
        lse_ref[0, rows, :] = lse_t


def _dil_attn(q3, k3, v3, gi, dilation):
    b, m, _ = q3.shape
    qr = DIL_QROWS
    sub = qr // DIL_BLOCK
    cur = pl.BlockSpec((1, qr, DIL_OUT_WIDTH), lambda i, r, n: (i, n, r))
    prev = pl.BlockSpec((1, DIL_BLOCK, DIL_OUT_WIDTH),
                        lambda i, r, n: (i, jnp.maximum(n * sub - 1, 0), r))
    return pl.pallas_call(
        _dil_attn_kernel,
        grid=(b, dilation, m // qr),
        in_specs=[cur, cur, prev, cur, prev],
        out_specs=[pl.BlockSpec((1, qr, DIL_OUT_WIDTH), lambda i, r, n: (i, n, r)),
                   pl.BlockSpec((1, qr, LANES), lambda i, r, n: (i, n, r))],
        out_shape=[jax.ShapeDtypeStruct((b, m, dilation * DIL_OUT_WIDTH), BF16),
                   jax.ShapeDtypeStruct((b, m, dilation * LANES), F32)],
        compiler_params=pltpu.CompilerParams(vmem_limit_bytes=VMEM_LIMIT_BYTES),
        name=f"dil_attn_g{gi}",
    )(q3, k3, k3, v3, v3)


def _mem_kv_kernel(mem_ref, nw_ref, w_ref, knw_ref, k_ref, v_ref):
    x = mem_ref[0]
    ms = jnp.mean(x * x, axis=-1, keepdims=True)
    u = (x * lax.rsqrt(ms + EPS) * nw_ref[...]).astype(BF16)
    width = MEM_HEADS * MEM_HEAD_PAD
    for h in range(MEM_HEADS):
        hs = slice(h * MEM_HEAD_PAD, (h + 1) * MEM_HEAD_PAD)
        kh = _dot(u, w_ref[:, hs])
        ss = jnp.sum(kh * kh, axis=-1, keepdims=True)
        k_ref[0, :, hs] = (kh * lax.rsqrt(ss / MEM_HEAD_DIM + EPS) * knw_ref[...]).astype(BF16)
        vs = slice(width + h * MEM_HEAD_PAD, width + (h + 1) * MEM_HEAD_PAD)
        v_ref[0, :, hs] = _dot(u, w_ref[:, vs]).astype(BF16)


def _mem_kv(mem, norm_w, w_kv, knw):
    b, n_mem, d = mem.shape
    width = MEM_HEADS * MEM_HEAD_PAD
    blk = pl.BlockSpec((1, n_mem, width), lambda i: (i, 0, 0))
    return pl.pallas_call(
        _mem_kv_kernel,
        grid=(b,),
        in_specs=[pl.BlockSpec((1, n_mem, d), lambda i: (i, 0, 0)), _resident((1, d)),
                  _resident(w_kv.shape), _resident((1, MEM_HEAD_PAD))],
        out_specs=[blk, blk],
        out_shape=[jax.ShapeDtypeStruct((b, n_mem, width), BF16)] * 2,
        compiler_params=pltpu.CompilerParams(vmem_limit_bytes=VMEM_LIMIT_BYTES),
        name="mem_kv",
    )(mem, norm_w, w_kv, knw)


MEM_QROWS = 512


def _mem_attn_kernel(q_ref, k_ref, v_ref, o_ref):
    for h in range(MEM_HEADS):
        hs = slice(h * MEM_HEAD_PAD, (h + 1) * MEM_HEAD_PAD)
        s = _dot_nt(q_ref[0, :, hs], k_ref[0, :, hs]) * (MEM_HEAD_DIM ** -0.5)
        m = jnp.max(s, axis=-1, keepdims=True)
        p = jnp.exp(s - m)
        l = jnp.sum(p, axis=-1, keepdims=True)
        o_ref[0, :, hs] = (_dot(p.astype(BF16), v_ref[0, :, hs]) / l).astype(BF16)


def _mem_attn(q, k, v):
    b, s, width = q.shape
    n_mem = k.shape[1]
    tm = MEM_QROWS
    kv = pl.BlockSpec((1, n_mem, width), lambda i, j: (i, 0, 0))
    return pl.pallas_call(
        _mem_attn_kernel,
        grid=(b, s // tm),
        in_specs=[pl.BlockSpec((1, tm, width), lambda i, j: (i, j, 0)), kv, kv],
        out_specs=pl.BlockSpec((1, tm, width), lambda i, j: (i, j, 0)),
        out_shape=jax.ShapeDtypeStruct((b, s, width), BF16),
        compiler_params=pltpu.CompilerParams(vmem_limit_bytes=VMEM_LIMIT_BYTES),
        name="mem_attn",
    )(q, k, v)


MERGE_ROWS = 512
MERGE_SLAB = 256


def _merge_kernel(x_ref, yssd_ref, o0_ref, o1_ref, o2_ref, l0_ref, l1_ref, l2_ref, ymem_ref,
                  gate_ref, wssd_ref, wdil_ref, wmem_ref, wout_ref, e4_ref, h_ref,
                  os0a_ref, os0b_ref, os1a_ref, os1b_ref, os2a_ref, os2b_ref,
                  ls0_ref, ls1_ref, ls2_ref):
    ostage = ((os0a_ref, os0b_ref), (os1a_ref, os1b_ref), (os2a_ref, os2b_ref))
    lstage = (ls0_ref, ls1_ref, ls2_ref)
    for gi, (o_ref, l_ref) in enumerate(((o0_ref, l0_ref), (o1_ref, l1_ref), (o2_ref, l2_ref))):
        dil = DIL_PAIRS[gi][1]
        n = MERGE_ROWS // dil
        for r in range(dil):
            tok = pl.ds(r, n, stride=dil) if dil > 1 else slice(None)
            for i in range(2):
                lo = r * DIL_OUT_WIDTH + i * LANES
                ostage[gi][i][tok, :] = o_ref[:, lo:lo + LANES].astype(F32)
            lstage[gi][tok, :] = l_ref[:, r * LANES:(r + 1) * LANES]
    for r0 in range(0, MERGE_ROWS, MERGE_SLAB):
        rows = slice(r0, r0 + MERGE_SLAB)
        l0, l1, l2 = ls0_ref[rows, :], ls1_ref[rows, :], ls2_ref[rows, :]
        lmax = jnp.maximum(jnp.maximum(l0, l1), l2)
        e0, e1, e2 = jnp.exp(l0 - lmax), jnp.exp(l1 - lmax), jnp.exp(l2 - lmax)
        inv = 1.0 / (e0 + e1 + e2)
        ydil = jnp.zeros((MERGE_SLAB, DIL_OUT_WIDTH), F32)
        for gi, e in enumerate((e0, e1, e2)):
            og = jnp.concatenate([ostage[gi][0][rows, :], ostage[gi][1][rows, :]], axis=1)
            ydil = ydil + _dot_f32(e * inv, e4_ref[...]) * og
        merged = (_sigmoid(gate_ref[rows, 0:D_MODEL].astype(F32))
                  * _dot(yssd_ref[rows, :], wssd_ref[...])
                  + _sigmoid(gate_ref[rows, D_MODEL:2 * D_MODEL].astype(F32))
                  * _dot(ydil.astype(BF16), wdil_ref[...])
                  + _sigmoid(gate_ref[rows, 2 * D_MODEL:3 * D_MODEL].astype(F32))
                  * _dot(ymem_ref[rows, :], wmem_ref[...]))
        h_ref[rows, :] = x_ref[rows, :] + _dot(merged.astype(BF16), wout_ref[...])


def _merge(x2d, yssd, o, lse, ymem, gate, wssd, wdil, wmem, wout, e4):
    m, d = x2d.shape
    tm = MERGE_ROWS
    row = lambda width, dil=1: pl.BlockSpec((tm // dil, width * dil), lambda i: (i, 0))
    dils = [dil for _, dil in DIL_PAIRS]
    ng = len(dils)
    return pl.pallas_call(
        _merge_kernel,
        grid=(m // tm,),
        in_specs=[row(d), row(d)] + [row(DIL_OUT_WIDTH, dil) for dil in dils]
                 + [row(LANES, dil) for dil in dils]
                 + [row(MEM_HEADS * MEM_HEAD_PAD), row(3 * d),
                    _resident(wssd.shape), _resident(wdil.shape), _resident(wmem.shape),
                    _resident(wout.shape), _resident(e4.shape)],
        out_specs=row(d),
        out_shape=jax.ShapeDtypeStruct((m, d), F32),
        scratch_shapes=[pltpu.VMEM((tm, LANES), F32)] * (3 * ng),
        compiler_params=pltpu.CompilerParams(vmem_limit_bytes=VMEM_LIMIT_BYTES),
        name="merge",
    )(x2d, yssd, o[0], o[1], o[2], lse[0], lse[1], lse[2], ymem, gate, wssd, wdil, wmem, wout, e4)


def _pad_heads(w, axis):
    shape = list(w.shape)
    shape[axis:axis + 1] = [MEM_HEADS, MEM_HEAD_DIM]
    w = w.reshape(shape)
    pad = [(0, 0)] * w.ndim
    pad[axis + 1] = (0, MEM_HEAD_PAD - MEM_HEAD_DIM)
    w = jnp.pad(w, pad)
    shape[axis:axis + 2] = [MEM_HEADS * MEM_HEAD_PAD]
    return w.reshape(shape)


def _mix_block(x, mem, positions, norm_mix_w, w_in, ssd_conv_w, ssd_conv_b, ssd_dt_bias,
               ssd_a_log, ssd_d, ssd_norm_w, dil_q_norm_w, dil_k_norm_w, mem_norm_w, w_mem_kv,
               mem_q_norm_w, mem_k_norm_w, w_ssd_br, w_dil_br, w_mem_br, w_out):
    b, s, d = x.shape
    m = b * s
    pad_lanes = lambda v: jnp.pad(v, (0, LANES - v.shape[0])).reshape(1, LANES)
    w_packed = jnp.concatenate([
        w_in[:, OFF_Z:OFF_DT],
        jnp.pad(w_in[:, OFF_DT:OFF_DQ], ((0, 0), (0, LANES - SSD_HEADS))),
        w_in[:, OFF_DQ:OFF_MQ],
        _pad_heads(w_in[:, OFF_MQ:OFF_GATE], 1),
        w_in[:, OFF_GATE:]], axis=1).astype(BF16)
    half = ROPE_DIMS // 2
    inv = jnp.exp(-math.log(ROPE_THETA) * (2.0 / ROPE_DIMS) * jnp.arange(half, dtype=F32))
    invp = jnp.tile(inv, LANES // half).reshape(1, LANES)
    pos_b = jnp.broadcast_to(positions.astype(F32).reshape(m, 1), (m, LANES))
    tile2 = lambda v: jnp.tile(v, 2).reshape(1, LANES)
    mqnw = jnp.pad(mem_q_norm_w, (0, MEM_HEAD_PAD - MEM_HEAD_DIM)).reshape(1, MEM_HEAD_PAD)
    mknw = jnp.pad(mem_k_norm_w, (0, MEM_HEAD_PAD - MEM_HEAD_DIM)).reshape(1, MEM_HEAD_PAD)
    z, xbc, dt_raw, q0, q1, q2, k0, k1, k2, v0, v1, v2, mq, gate = _in_proj(
        x.reshape(m, d), pos_b, norm_mix_w.reshape(1, d), w_packed,
        tile2(dil_q_norm_w), tile2(dil_k_norm_w), mqnw, invp)

    expand = jnp.repeat(jnp.eye(LANES, SSD_HEADS, dtype=BF16), SSD_HEAD_DIM, axis=1)
    y_ssd = _ssd(xbc.reshape(b, s, -1), z.reshape(b, s, -1), dt_raw.reshape(b, s, -1),
                 ssd_conv_w, ssd_conv_b.reshape(1, -1), pad_lanes(ssd_dt_bias), pad_lanes(ssd_a_log),
                 jnp.repeat(ssd_d, SSD_HEAD_DIM).reshape(1, d), ssd_norm_w.reshape(1, d), expand)

    outs, lses = [], []
    for gi, (window, dilation) in enumerate(DIL_PAIRS):
        assert window // dilation == DIL_BLOCK
        per_batch = lambda t: t.reshape(b, s // dilation, -1)
        o, lse = _dil_attn(per_batch((q0, q1, q2)[gi]), per_batch((k0, k1, k2)[gi]),
                           per_batch((v0, v1, v2)[gi]), gi, dilation)
        outs.append(o.reshape(m // dilation, -1))
        lses.append(lse.reshape(m // dilation, -1))

    w_kv = jnp.concatenate([_pad_heads(w_mem_kv[:, :MEM_WIDTH], 1),
                            _pad_heads(w_mem_kv[:, MEM_WIDTH:], 1)], axis=1).astype(BF16)
    mk, mv = _mem_kv(mem, mem_norm_w.reshape(1, d), w_kv, mknw)
    y_mem = _mem_attn(mq.reshape(b, s, -1), mk, mv).reshape(m, -1)

    e4 = jnp.repeat(jnp.eye(LANES, DIL_OUT_WIDTH // DIL_HEAD_DIM, dtype=F32), DIL_HEAD_DIM, axis=1)
    return _merge(x.reshape(m, d), y_ssd.reshape(m, d), outs, lses, y_mem, gate,
                  w_ssd_br.astype(BF16), w_dil_br.astype(BF16),
                  _pad_heads(w_mem_br, 0).astype(BF16), w_out.astype(BF16), e4)


PEER_TOKENS = 512
PEER_A_PER_STEP = 8
ROUTE_LANES = 128


def _oddeven_merge(lo, hi, r):
    step = r * 2
    if step < hi - lo:
        yield from _oddeven_merge(lo, hi, step)
        yield from _oddeven_merge(lo + r, hi, step)
        yield from [(i, i + r) for i in range(lo + r, hi - r, step)]
    else:
        yield (lo, lo + r)


def _oddeven_merge_sort(lo, hi):
    if hi - lo >= 1:
        mid = lo + (hi - lo) // 2
        yield from _oddeven_merge_sort(lo, mid)
        yield from _oddeven_merge_sort(mid + 1, hi)
        yield from _oddeven_merge(lo, hi, 1)


_SORT16 = tuple(_oddeven_merge_sort(0, PEER_TOPK - 1))


def _cmpx(a, i, j):
    hi = jnp.maximum(a[i], a[j])
    lo = jnp.minimum(a[i], a[j])
    a[i], a[j] = hi, lo


def _sort16_desc(a):
    for i, j in _SORT16:
        _cmpx(a, i, j)
    return a


def _bitonic16_desc(a):
    for d in (8, 4, 2, 1):
        for i in range(PEER_TOPK):
            if not i & d:
                _cmpx(a, i, i + d)
    return a


def _merge_top16(a, b):
    c = [jnp.maximum(a[k], b[PEER_TOPK - 1 - k]) for k in range(PEER_TOPK)]
    return _bitonic16_desc(c)


def _merge_sublanes(a):
    for sh in (4, 2, 1):
        b = [pltpu.roll(x, sh, axis=0) for x in a]
        a = _merge_top16(a, b)
    return a


def _count_prefix(rows, test):
    def pick(bits, lo):
        if not bits:
            return rows[lo]
        (c, stride), rest = bits[0], bits[1:]
        return jnp.where(c, pick(rest, lo + stride), pick(rest, lo))

    bits = []
    for stride in (8, 4, 2, 1):
        bits.append((test(pick(bits, stride - 1)), stride))
    count = None
    for c, stride in bits:
        term = jnp.where(c, float(stride), 0.0)
        count = term if count is None else count + term
    return jnp.where(test(rows[PEER_TOPK - 1]), float(PEER_TOPK), count)


def _top16_rows(s):
    rows = [s[8 * i:8 * i + 8, :] for i in range(PEER_N_KEYS // 8)]
    return _merge_sublanes(_sort16_desc(rows))


def _peer_route_kernel(h_ref, nw_ref, wqt_ref, keys_ref,
                       hnt_ref, cnt_ref, f1_ref, rank_ref, e2_ref, s1_ref, s2_ref):
    t = h_ref.shape[0]
    x = h_ref[...]
    ms = jnp.mean(x * x, axis=-1, keepdims=True)
    hn = x * lax.rsqrt(ms + EPS) * nw_ref[...]
    hnt = hn.T.astype(BF16)
    hnt_ref[...] = hnt
    ch = ROUTE_LANES
    sub = lax.broadcasted_iota(jnp.int32, (8, ch), 0)

    def distribute(vs):
        out = vs[7]
        for r in range(6, -1, -1):
            out = jnp.where(sub == r, vs[r], out)
        return out

    for h in range(PEER_HEADS):
        qt = jnp.dot(wqt_ref[h * PEER_QUERY_DIM:(h + 1) * PEER_QUERY_DIM, :], hnt,
                     preferred_element_type=F32).astype(BF16)
        half = PEER_QUERY_DIM // 2
        s1_ref[...] = jnp.dot(keys_ref[h, 0], qt[:half], preferred_element_type=F32)
        s2_ref[...] = jnp.dot(keys_ref[h, 1], qt[half:], preferred_element_type=F32)
        for c0 in range(0, t, ch):
            cols = slice(c0, c0 + ch)
            s1 = s1_ref[:, cols]
            s2 = s2_ref[:, cols]
            v1 = _top16_rows(s1)
            v2 = _top16_rows(s2)
            a_lo = distribute(v1[:8])
            a_hi = distribute(v1[8:])
            lo = [a_lo + v2[j] for j in range(PEER_TOPK)]
            hi = [a_hi + v2[j] for j in range(PEER_TOPK)]
            tau = _merge_sublanes(_merge_top16(lo, hi))[PEER_TOPK - 1]
            e_lo = jnp.exp(a_lo - v1[0])
            e_hi = jnp.exp(a_hi - v1[0])
            zp = jnp.zeros((8, ch), F32)
            for j in range(PEER_TOPK):
                e2j = jnp.exp(v2[j] - v2[0])
                zp = zp + jnp.where(lo[j] >= tau, e_lo * e2j, 0.0)
                zp = zp + jnp.where(hi[j] >= tau, e_hi * e2j, 0.0)
            z = jnp.sum(zp, axis=0, keepdims=True)
            tau_r = tau[0:1, :]
            v2r = [v[0:1, :] for v in v2]
            cnt = _count_prefix(v2r, lambda r: s1 + r >= tau_r)
            rank = _count_prefix(v2r, lambda r: r > s2)
            cnt_ref[h, :, cols] = cnt
            f1_ref[h, :, cols] = jnp.exp(s1 - v1[0][0:1, :]) * (1.0 / z)
            rank_ref[h, :, cols] = rank.astype(BF16)
            e2_ref[h, :, cols] = jnp.exp(s2 - v2[0][0:1, :]).astype(BF16)


def _peer_route(h2d, norm_w, wqt, keys):
    n, d = h2d.shape
    t = PEER_TOKENS
    route = lambda dt: jax.ShapeDtypeStruct((PEER_HEADS, PEER_N_KEYS, n), dt)
    route_spec = pl.BlockSpec((PEER_HEADS, PEER_N_KEYS, t), lambda i: (0, 0, i))
    return pl.pallas_call(
        _peer_route_kernel,
        grid=(n // t,),
        in_specs=[
            pl.BlockSpec((t, d), lambda i: (i, 0)),
            pl.BlockSpec((1, d), lambda i: (0, 0)),
            pl.BlockSpec(wqt.shape, lambda i: (0, 0)),
            pl.BlockSpec(keys.shape, lambda i: (0, 0, 0, 0)),
        ],
        out_specs=[pl.BlockSpec((d, t), lambda i: (0, i)),
                   route_spec, route_spec, route_spec, route_spec],
        out_shape=[jax.ShapeDtypeStruct((d, n), BF16), route(F32), route(F32), route(BF16), route(BF16)],
        scratch_shapes=[pltpu.VMEM((PEER_N_KEYS, t), F32), pltpu.VMEM((PEER_N_KEYS, t), F32)],
        compiler_params=pltpu.CompilerParams(vmem_limit_bytes=VMEM_LIMIT_BYTES),
        name="peer_route",
    )(h2d, norm_w.reshape(1, d), wqt, keys)


BF16_ROWS = 16


def _peer_expert_kernel(h_ref, hnt_ref, cnt_ref, f1_ref, rank_ref, e2_ref,
                        down_ref, upt_ref, o_ref, acc_ref, coef_ref, w_ref, cntb_ref, f1b_ref):
    step = pl.program_id(1)
    n_blocks = pl.num_programs(1) - 1
    t = hnt_ref.shape[1]
    cur = step % 2

    @pl.when(step == 0)
    def _():
        acc_ref[...] = jnp.zeros_like(acc_ref)
        coef_ref[1] = jnp.zeros_like(coef_ref[1])

    block = jnp.minimum(step, n_blocks - 1)
    for q in range(PEER_A_PER_STEP):
        a = block * PEER_A_PER_STEP + q
        for h in range(PEER_HEADS):
            i = q * PEER_HEADS + h
            cntb_ref[i] = jnp.broadcast_to(cnt_ref[h, pl.ds(a, 1), :], (BF16_ROWS, t)).astype(BF16)
            f1b_ref[i] = jnp.broadcast_to(f1_ref[h, pl.ds(a, 1), :], (BF16_ROWS, t)).astype(BF16)
    for c0 in range(0, t, 2 * LANES):
        cols = slice(c0, c0 + 2 * LANES)
        for r0 in range(0, PEER_N_KEYS, BF16_ROWS):
            rows = slice(r0, r0 + BF16_ROWS)
            ranks = [rank_ref[h, rows, cols] for h in range(PEER_HEADS)]
            e2s = [e2_ref[h, rows, cols] for h in range(PEER_HEADS)]
            for q in range(PEER_A_PER_STEP):
                w = None
                for h in range(PEER_HEADS):
                    i = q * PEER_HEADS + h
                    wh = jnp.where(ranks[h] < cntb_ref[i, :, cols], e2s[h],
                                   jnp.zeros_like(e2s[h])) * f1b_ref[i, :, cols]
                    w = wh if w is None else w + wh
                er = slice(q * PEER_N_KEYS + r0, q * PEER_N_KEYS + r0 + BF16_ROWS)
                w_ref[er, cols] = w

    acc_ref[...] += jnp.dot(upt_ref[...], coef_ref[1 - cur], preferred_element_type=F32)
    s = jnp.dot(down_ref[...], hnt_ref[...], preferred_element_type=F32)
    sb = s.astype(BF16)
    act = (0.5 * sb) * (1.0 + lax.erf(sb * (2.0 ** -0.5)))
    coef_ref[cur] = w_ref[...] * act

    @pl.when(step == n_blocks)
    def _():
        o_ref[...] = h_ref[...] + acc_ref[...].T


def _peer_expert(h2d, hnt, cnt, f1, rank, e2, down_bf, upt_bf):
    n, d = h2d.shape
    t = PEER_TOKENS
    eb = PEER_A_PER_STEP * PEER_N_KEYS
    pairs = PEER_A_PER_STEP * PEER_HEADS
    route_spec = pl.BlockSpec((PEER_HEADS, PEER_N_KEYS, t), lambda i, j: (0, 0, i))
    n_blocks = PEER_N_KEYS // PEER_A_PER_STEP
    return pl.pallas_call(
        _peer_expert_kernel,
        grid=(n // t, n_blocks + 1),
        in_specs=[
            pl.BlockSpec((t, d), lambda i, j: (i, 0)),
            pl.BlockSpec((d, t), lambda i, j: (0, i)),
            route_spec, route_spec, route_spec, route_spec,
            pl.BlockSpec((eb, d), lambda i, j: (jnp.minimum(j, n_blocks - 1), 0)),
            pl.BlockSpec((d, eb), lambda i, j: (0, jnp.maximum(j - 1, 0))),
        ],
        out_specs=pl.BlockSpec((t, d), lambda i, j: (i, 0)),
        out_shape=jax.ShapeDtypeStruct((n, d), F32),
        scratch_shapes=[pltpu.VMEM((d, t), F32), pltpu.VMEM((2, eb, t), BF16), pltpu.VMEM((eb, t), BF16),
                        pltpu.VMEM((pairs, BF16_ROWS, t), BF16), pltpu.VMEM((pairs, BF16_ROWS, t), BF16)],
        compiler_params=pltpu.CompilerParams(
            dimension_semantics=("arbitrary", "arbitrary"),
            vmem_limit_bytes=VMEM_LIMIT_BYTES),
        name="peer_expert",
    )(h2d, hnt, cnt, f1, rank, e2, down_bf, upt_bf)


def _peer_block(h2d, norm_w, w_query, sub_keys, expert_down, expert_up):
    wqt = w_query.T.astype(BF16)
    hnt, cnt, f1, rank, e2 = _peer_route(h2d, norm_w, wqt, sub_keys.astype(BF16))
    return _peer_expert(h2d, hnt, cnt, f1, rank, e2,
                        expert_down.astype(BF16), expert_up.T.astype(BF16))


def kernel(x, mem, positions, norm_mix_w, w_in, ssd_conv_w, ssd_conv_b, ssd_dt_bias,
           ssd_a_log, ssd_d, ssd_norm_w, dil_q_norm_w, dil_k_norm_w, mem_norm_w, w_mem_kv,
           mem_q_norm_w, mem_k_norm_w, w_ssd_br, w_dil_br, w_mem_br, w_out, norm_ffn_w,
           peer_w_query, peer_sub_keys, peer_down, peer_up):
    b, s, d = x.shape
    assert norm_mix_w.shape[0] == 1, "single-layer block"
    h = _mix_block(x, mem, positions, norm_mix_w[0], w_in[0], ssd_conv_w[0], ssd_conv_b[0],
                   ssd_dt_bias[0], ssd_a_log[0], ssd_d[0], ssd_norm_w[0], dil_q_norm_w[0],
                   dil_k_norm_w[0], mem_norm_w[0], w_mem_kv[0], mem_q_norm_w[0], mem_k_norm_w[0],
                   w_ssd_br[0], w_dil_br[0], w_mem_br[0], w_out[0])
    h = _peer_block(h, norm_ffn_w[0], peer_w_query[0], peer_sub_keys[0], peer_down[0], peer_up[0])
    return h.reshape(b, s, d)
```

```python
import math

import jax
import jax.numpy as jnp
from jax import lax
from jax.experimental import pallas as pl
from jax.experimental.pallas import tpu as pltpu

D_MODEL = 1024
EPS = 1e-6
SSD_HEAD_DIM = 64
SSD_HEADS = 16
SSD_GROUPS = 4
SSD_HEADS_PER_GROUP = 4
SSD_STATE = 128
SSD_CONV = 4
SSD_CHUNK = 128
SSD_CONV_DIM = 2048
DIL_PAIRS = ((128, 1), (512, 4), (2048, 16))
DIL_HEAD_DIM = 64
DIL_WIDTH = 768
DIL_OUT_WIDTH = 256
MEM_HEADS = 4
MEM_HEAD_DIM = 192
MEM_HEAD_PAD = 256
MEM_WIDTH = 768
ROPE_THETA = 500000.0
ROPE_DIMS = 16
PEER_HEADS = 8
PEER_N_KEYS = 128
PEER_QUERY_DIM = 256
PEER_TOPK = 16
OFF_Z = 0
OFF_XBC = 1024
OFF_DT = 3072
OFF_DQ = 3088
OFF_DK = OFF_DQ + DIL_WIDTH
OFF_DV = OFF_DK + DIL_WIDTH
OFF_MQ = OFF_DV + DIL_WIDTH
OFF_GATE = OFF_MQ + MEM_WIDTH
F32 = jnp.float32
BF16 = jnp.bfloat16
HIGHEST = lax.Precision.HIGHEST
LANES = 128
VMEM_LIMIT_BYTES = 56 * 1024 * 1024

P_Z = 0
P_XBC = 1024
P_DT = 3072
P_Q = 3200
P_K = P_Q + DIL_WIDTH
P_V = P_K + DIL_WIDTH
P_MQ = P_V + DIL_WIDTH
P_GATE = P_MQ + MEM_HEADS * MEM_HEAD_PAD
P_WIDTH = P_GATE + 3 * D_MODEL


def _resident(shape):
    nd = len(shape)
    return pl.BlockSpec(shape, lambda *_: (0,) * nd, pipeline_mode=pl.Buffered(1))


def _sigmoid(x):
    return 1.0 / (1.0 + jnp.exp(-x))


def _dot(a, b):
    return jnp.dot(a, b, preferred_element_type=F32)


def _dot_nt(a, b):
    return lax.dot_general(a, b, (((1,), (1,)), ((), ())), preferred_element_type=F32)


def _dot_f32(a, b):
    return jnp.dot(a, b, precision=HIGHEST, preferred_element_type=F32)


IN_PROJ_ROWS = 512
PLAIN_CHUNK = 256


def _in_proj_kernel(x_ref, pos_ref, nw_ref, w_ref, qnw_ref, knw_ref, mqnw_ref, invp_ref,
                    z_ref, xbc_ref, dt_ref, q0_ref, q1_ref, q2_ref, k0_ref, k1_ref, k2_ref,
                    v0_ref, v1_ref, v2_ref, mq_ref, gate_ref, u_ref, stage_a_ref, stage_b_ref):
    x = x_ref[...]
    ms = jnp.mean(x * x, axis=-1, keepdims=True)
    u_ref[...] = (x * lax.rsqrt(ms + EPS) * nw_ref[...]).astype(BF16)

    def seg(off, width):
        return _dot(u_ref[...], w_ref[:, off:off + width])

    for j in range(0, 1024, PLAIN_CHUNK):
        z_ref[:, j:j + PLAIN_CHUNK] = seg(P_Z + j, PLAIN_CHUNK).astype(BF16)
    for j in range(0, SSD_CONV_DIM, PLAIN_CHUNK):
        xbc_ref[:, j:j + PLAIN_CHUNK] = seg(P_XBC + j, PLAIN_CHUNK).astype(BF16)
    dt_ref[...] = seg(P_DT, LANES)
    rows = x.shape[0]

    stages = (stage_a_ref, stage_b_ref)

    def put_group(gi, o_ref):
        dil = DIL_PAIRS[gi][1]
        for i, st in enumerate(stages):
            for r in range(dil):
                tok = pl.ds(r, rows // dil, stride=dil) if dil > 1 else slice(None)
                lo = r * DIL_OUT_WIDTH + i * LANES
                o_ref[:, lo:lo + LANES] = st[tok, :].astype(BF16)

    for gi, o_ref in enumerate((v0_ref, v1_ref, v2_ref)):
        vt = seg(P_V + gi * DIL_OUT_WIDTH, DIL_OUT_WIDTH)
        for i, st in enumerate(stages):
            st[...] = vt[:, i * LANES:(i + 1) * LANES]
        put_group(gi, o_ref)
    for j in range(0, 3 * D_MODEL, PLAIN_CHUNK):
        gate_ref[:, j:j + PLAIN_CHUNK] = seg(P_GATE + j, PLAIN_CHUNK).astype(BF16)

    ang = pos_ref[...] * invp_ref[...]
    cos = jnp.cos(ang)
    sin = jnp.sin(ang)
    lane = lax.broadcasted_iota(jnp.int32, (rows, LANES), 1)
    l64 = lane & (DIL_HEAD_DIM - 1)
    first = l64 < ROPE_DIMS // 2
    rotated = l64 < ROPE_DIMS
    cm = jnp.where(rotated, cos, 1.0)
    sm = jnp.where(rotated, jnp.where(first, -sin, sin), 0.0)
    left = lane < DIL_HEAD_DIM

    def qk_norm_rope(off, w_row_ref, o_refs):
        for gi, o_ref in enumerate(o_refs):
            t2 = seg(off + gi * DIL_OUT_WIDTH, DIL_OUT_WIDTH)
            for i in range(2):
                t = t2[:, i * LANES:(i + 1) * LANES]
                sq = t * t
                ssl = jnp.sum(jnp.where(left, sq, 0.0), axis=-1, keepdims=True)
                ssr = jnp.sum(jnp.where(left, 0.0, sq), axis=-1, keepdims=True)
                r = jnp.where(left, lax.rsqrt(ssl / DIL_HEAD_DIM + EPS),
                              lax.rsqrt(ssr / DIL_HEAD_DIM + EPS))
                tn = t * r * w_row_ref[...]
                rot = jnp.where(first, pltpu.roll(tn, LANES - ROPE_DIMS // 2, axis=1),
                                pltpu.roll(tn, ROPE_DIMS // 2, axis=1))
                stages[i][...] = tn * cm + rot * sm
            put_group(gi, o_ref)

    qk_norm_rope(P_Q, qnw_ref, (q0_ref, q1_ref, q2_ref))
    qk_norm_rope(P_K, knw_ref, (k0_ref, k1_ref, k2_ref))

    for h in range(MEM_HEADS):
        t2 = seg(P_MQ + h * MEM_HEAD_PAD, MEM_HEAD_PAD)
        ss = jnp.sum(t2 * t2, axis=-1, keepdims=True)
        mq_ref[:, h * MEM_HEAD_PAD:(h + 1) * MEM_HEAD_PAD] = (
            t2 * lax.rsqrt(ss / MEM_HEAD_DIM + EPS) * mqnw_ref[...]).astype(BF16)


def _in_proj(x2d, pos_b, norm_w, w_packed, qnw, knw, mqnw, invp):
    m, d = x2d.shape
    tm = IN_PROJ_ROWS
    row = lambda width, dil=1: pl.BlockSpec((tm // dil, width * dil), lambda i: (i, 0))
    dil_outs = [(DIL_OUT_WIDTH, BF16, dil) for _, dil in DIL_PAIRS]
    outs = ([(1024, BF16, 1), (SSD_CONV_DIM, BF16, 1), (LANES, F32, 1)] + dil_outs * 3
            + [(MEM_HEADS * MEM_HEAD_PAD, BF16, 1), (3 * D_MODEL, BF16, 1)])
    return pl.pallas_call(
        _in_proj_kernel,
        grid=(m // tm,),
        in_specs=[row(d), row(LANES), _resident((1, d)), _resident(w_packed.shape),
                  _resident((1, LANES)), _resident((1, LANES)), _resident((1, MEM_HEAD_PAD)),
                  _resident((1, LANES))],
        out_specs=[row(w, dil) for w, _, dil in outs],
        out_shape=[jax.ShapeDtypeStruct((m // dil, w * dil), dt) for w, dt, dil in outs],
        scratch_shapes=[pltpu.VMEM((tm, d), BF16), pltpu.VMEM((tm, LANES), F32),
                        pltpu.VMEM((tm, LANES), F32)],
        compiler_params=pltpu.CompilerParams(vmem_limit_bytes=VMEM_LIMIT_BYTES),
        name="in_proj",
    )(x2d, pos_b, norm_w, w_packed, qnw, knw, mqnw, invp)


def _ssd_kernel(xbc_ref, z_ref, dt_ref, cw_ref, cb_ref, dtb_ref, alog_ref, dfull_ref, nw_ref,
                e_ref, y_ref, tail_ref, state_ref):
    c = pl.program_id(1)
    L = SSD_CHUNK

    @pl.when(c == 0)
    def _():
        tail_ref[...] = jnp.zeros_like(tail_ref)
        state_ref[...] = jnp.zeros_like(state_ref)

    cur = xbc_ref[0].astype(F32)
    xp = jnp.concatenate([tail_ref[...], cur], axis=0)
    conv = cb_ref[...] + cw_ref[SSD_CONV - 1:SSD_CONV, :] * cur
    for k in range(SSD_CONV - 1):
        lo = 8 - (SSD_CONV - 1) + k
        conv = conv + cw_ref[k:k + 1, :] * xp[lo:lo + L]
    tail_ref[...] = cur[L - 8:L]
    xa = conv * _sigmoid(conv)
    xs = xa[:, :D_MODEL]
    gn = SSD_GROUPS * SSD_STATE
    bm = xa[:, D_MODEL:D_MODEL + gn].astype(BF16)
    cm = xa[:, D_MODEL + gn:].astype(BF16)

    pre = dt_ref[0] + dtb_ref[...]
    dt = jnp.maximum(pre, 0.0) + jnp.log(1.0 + jnp.exp(-jnp.abs(pre)))
    da = dt * (-jnp.exp(alog_ref[...]))
    ri = lax.broadcasted_iota(jnp.int32, (L, L), 0)
    ci = lax.broadcasted_iota(jnp.int32, (L, L), 1)
    tril = ri >= ci
    acum = _dot_f32(tril.astype(F32), da)
    acum_t = acum.T
    def expand(v):
        hi = v.astype(BF16)
        lo = (v - hi.astype(F32)).astype(BF16)
        return _dot(hi, e_ref[...]) + _dot(lo, e_ref[...])

    dt_full = expand(dt)
    ea_full = expand(jnp.exp(acum))
    dec_full = expand(jnp.exp(acum[L - 1:L, :] - acum))
    xdt = xs * dt_full
    xdt_b = xdt.astype(BF16)
    xdd_b = (xdt * dec_full).astype(BF16)
    gw = SSD_HEADS_PER_GROUP * SSD_HEAD_DIM
    head_of_lane = lax.broadcasted_iota(jnp.int32, (L, gw), 1) // SSD_HEAD_DIM
    ys = []
    for g in range(SSD_GROUPS):
        bg = bm[:, g * SSD_STATE:(g + 1) * SSD_STATE]
        cg = cm[:, g * SSD_STATE:(g + 1) * SSD_STATE]
        gs = slice(g * gw, (g + 1) * gw)
        cb = _dot_nt(cg, bg)
        st_old = state_ref[g]
        yg = _dot(cg, st_old.astype(BF16)) * ea_full[:, gs]
        ms = []
        for r in range(SSD_HEADS_PER_GROUP):
            h = g * SSD_HEADS_PER_GROUP + r
            seg = acum[:, h:h + 1] - acum_t[h:h + 1, :]
            lmat = jnp.exp(jnp.where(tril, seg, -jnp.inf))
            ms.append((cb * lmat).astype(BF16))
        yd = _dot(jnp.concatenate(ms, axis=0), xdt_b[:, gs])
        for r in range(SSD_HEADS_PER_GROUP):
            yg = yg + jnp.where(head_of_lane == r, yd[r * L:(r + 1) * L], 0.0)
        ys.append(yg)
        s_new = lax.dot_general(bg, xdd_b[:, gs], (((0,), (0,)), ((), ())),
                                preferred_element_type=F32)
        state_ref[g] = st_old * ea_full[L - 1:L, gs] + s_new
    y = jnp.concatenate(ys, axis=1) + dfull_ref[...] * xs
    zf = z_ref[0].astype(F32)
    y = y * (zf * _sigmoid(zf))
    outs = []
    for g in range(SSD_GROUPS):
        yg = y[:, g * gw:(g + 1) * gw]
        ss = jnp.sum(yg * yg, axis=-1, keepdims=True)
        outs.append(yg * lax.rsqrt(ss / gw + EPS))
    y_ref[0] = (jnp.concatenate(outs, axis=1) * nw_ref[...]).astype(BF16)


def _ssd(xbc, z, dt_raw, conv_w, conv_b, dt_bias, a_log, d_full, norm_w, expand):
    b, s, _ = xbc.shape
    L = SSD_CHUNK
    blk = lambda width: pl.BlockSpec((1, L, width), lambda i, c: (i, c, 0))
    return pl.pallas_call(
        _ssd_kernel,
        grid=(b, s // L),
        in_specs=[blk(SSD_CONV_DIM), blk(D_MODEL), blk(LANES),
                  _resident(conv_w.shape), _resident(conv_b.shape), _resident(dt_bias.shape),
                  _resident(a_log.shape), _resident(d_full.shape), _resident(norm_w.shape),
                  _resident(expand.shape)],
        out_specs=blk(D_MODEL),
        out_shape=jax.ShapeDtypeStruct((b, s, D_MODEL), BF16),
        scratch_shapes=[pltpu.VMEM((8, SSD_CONV_DIM), F32),
                        pltpu.VMEM((SSD_GROUPS, SSD_STATE, SSD_HEADS_PER_GROUP * SSD_HEAD_DIM), F32)],
        compiler_params=pltpu.CompilerParams(
            dimension_semantics=("arbitrary", "arbitrary"), vmem_limit_bytes=VMEM_LIMIT_BYTES),
        name="ssd",
    )(xbc, z, dt_raw, conv_w, conv_b, dt_bias, a_log, d_full, norm_w, expand)


DIL_BLOCK = 128
DIL_QROWS = 512


def _dil_attn_kernel(q_ref, kc_ref, kp_ref, vc_ref, vp_ref, o_ref, lse_ref):
    n = pl.program_id(2)
    blk = DIL_BLOCK
    w = DIL_OUT_WIDTH
    nh = w // DIL_HEAD_DIM
    head_of_lane = lax.broadcasted_iota(jnp.int32, (blk, w), 1) // DIL_HEAD_DIM
    lane = lax.broadcasted_iota(jnp.int32, (blk, LANES), 1)
    qi = lax.broadcasted_iota(jnp.int32, (nh * blk, 2 * blk), 0) % blk
    ki = lax.broadcasted_iota(jnp.int32, (nh * blk, 2 * blk), 1)
    band = (ki >= qi) & (ki <= qi + blk)
    first_valid = band & (ki + jnp.minimum(n, 1) * blk >= blk)
    for sb in range(DIL_QROWS // blk):
        rows = slice(sb * blk, (sb + 1) * blk)
        q = q_ref[0, rows, :]
        if sb == 0:
            kprev, vprev, valid = kp_ref[0], vp_ref[0], first_valid
        else:
            prev = slice((sb - 1) * blk, sb * blk)
            kprev, vprev, valid = kc_ref[0, prev, :], vc_ref[0, prev, :], band
        kk = jnp.concatenate([kprev, kc_ref[0, rows, :]], axis=0)
        vv = jnp.concatenate([vprev, vc_ref[0, rows, :]], axis=0)
        qs = jnp.concatenate([jnp.where(head_of_lane == h, q, jnp.zeros_like(q))
                              for h in range(nh)], axis=0)
        s = _dot_nt(qs, kk) * (DIL_HEAD_DIM ** -0.5)
        s = jnp.where(valid, s, -jnp.inf)
        m = jnp.max(s, axis=-1, keepdims=True)
        p = jnp.exp(s - m)
        l = jnp.sum(p, axis=-1, keepdims=True)
        on = _dot(p.astype(BF16), vv) / l
        lse = m + jnp.log(l)
        o_acc = jnp.zeros((blk, w), F32)
        lse_t = jnp.zeros((blk, LANES), F32)
        for h in range(nh):
            hr = slice(h * blk, (h + 1) * blk)
            o_acc = jnp.where(head_of_lane == h, on[hr], o_acc)
            lse_t = jnp.where(lane == h, lse[hr], lse_t)
        o_ref[0, rows, :] = o_acc.astype(BF16)
        lse_ref[0, rows, :] = lse_t


def _dil_attn(q3, k3, v3, gi, dilation):
    b, m, _ = q3.shape
    qr = DIL_QROWS
    sub = qr // DIL_BLOCK
    cur = pl.BlockSpec((1, qr, DIL_OUT_WIDTH), lambda i, r, n: (i, n, r))
    prev = pl.BlockSpec((1, DIL_BLOCK, DIL_OUT_WIDTH),
                        lambda i, r, n: (i, jnp.maximum(n * sub - 1, 0), r))
    return pl.pallas_call(
        _dil_attn_kernel,
        grid=(b, dilation, m // qr),
        in_specs=[cur, cur, prev, cur, prev],
        out_specs=[pl.BlockSpec((1, qr, DIL_OUT_WIDTH), lambda i, r, n: (i, n, r)),
                   pl.BlockSpec((1, qr, LANES), lambda i, r, n: (i, n, r))],
        out_shape=[jax.ShapeDtypeStruct((b, m, dilation * DIL_OUT_WIDTH), BF16),
                   jax.ShapeDtypeStruct((b, m, dilation * LANES), F32)],
        compiler_params=pltpu.CompilerParams(vmem_limit_bytes=VMEM_LIMIT_BYTES),
        name=f"dil_attn_g{gi}",
    )(q3, k3, k3, v3, v3)


def _mem_kv_kernel(mem_ref, nw_ref, w_ref, knw_ref, k_ref, v_ref):
    x = mem_ref[0]
    ms = jnp.mean(x * x, axis=-1, keepdims=True)
    u = (x * lax.rsqrt(ms + EPS) * nw_ref[...]).astype(BF16)
    width = MEM_HEADS * MEM_HEAD_PAD
    for h in range(MEM_HEADS):
        hs = slice(h * MEM_HEAD_PAD, (h + 1) * MEM_HEAD_PAD)
        kh = _dot(u, w_ref[:, hs])
        ss = jnp.sum(kh * kh, axis=-1, keepdims=True)
        k_ref[0, :, hs] = (kh * lax.rsqrt(ss / MEM_HEAD_DIM + EPS) * knw_ref[...]).astype(BF16)
        vs = slice(width + h * MEM_HEAD_PAD, width + (h + 1) * MEM_HEAD_PAD)
        v_ref[0, :, hs] = _dot(u, w_ref[:, vs]).astype(BF16)


def _mem_kv(mem, norm_w, w_kv, knw):
    b, n_mem, d = mem.shape
    width = MEM_HEADS * MEM_HEAD_PAD
    blk = pl.BlockSpec((1, n_mem, width), lambda i: (i, 0, 0))
    return pl.pallas_call(
        _mem_kv_kernel,
        grid=(b,),
        in_specs=[pl.BlockSpec((1, n_mem, d), lambda i: (i, 0, 0)), _resident((1, d)),
                  _resident(w_kv.shape), _resident((1, MEM_HEAD_PAD))],
        out_specs=[blk, blk],
        out_shape=[jax.ShapeDtypeStruct((b, n_mem, width), BF16)] * 2,
        compiler_params=pltpu.CompilerParams(vmem_limit_bytes=VMEM_LIMIT_BYTES),
        name="mem_kv",
    )(mem, norm_w, w_kv, knw)


MEM_QROWS = 512


def _mem_attn_kernel(q_ref, k_ref, v_ref, o_ref):
    for h in range(MEM_HEADS):
        hs = slice(h * MEM_HEAD_PAD, (h + 1) * MEM_HEAD_PAD)
        s = _dot_nt(q_ref[0, :, hs], k_ref[0, :, hs]) * (MEM_HEAD_DIM ** -0.5)
        m = jnp.max(s, axis=-1, keepdims=True)
        p = jnp.exp(s - m)
        l = jnp.sum(p, axis=-1, keepdims=True)
        o_ref[0, :, hs] = (_dot(p.astype(BF16), v_ref[0, :, hs]) / l).astype(BF16)


def _mem_attn(q, k, v):
    b, s, width = q.shape
    n_mem = k.shape[1]
    tm = MEM_QROWS
    kv = pl.BlockSpec((1, n_mem, width), lambda i, j: (i, 0, 0))
    return pl.pallas_call(
        _mem_attn_kernel,
        grid=(b, s // tm),
        in_specs=[pl.BlockSpec((1, tm, width), lambda i, j: (i, j, 0)), kv, kv],
        out_specs=pl.BlockSpec((1, tm, width), lambda i, j: (i, j, 0)),
        out_shape=jax.ShapeDtypeStruct((b, s, width), BF16),
        compiler_params=pltpu.CompilerParams(vmem_limit_bytes=VMEM_LIMIT_BYTES),
        name="mem_attn",
    )(q, k, v)


MERGE_ROWS = 512
MERGE_SLAB = 256


def _merge_kernel(x_ref, yssd_ref, o0_ref, o1_ref, o2_ref, l0_ref, l1_ref, l2_ref, ymem_ref,
                  gate_ref, wssd_ref, wdil_ref, wmem_ref, wout_ref, e4_ref, h_ref,
                  os0a_ref, os0b_ref, os1a_ref, os1b_ref, os2a_ref, os2b_ref,
                  ls0_ref, ls1_ref, ls2_ref):
    ostage = ((os0a_ref, os0b_ref), (os1a_ref, os1b_ref), (os2a_ref, os2b_ref))
    lstage = (ls0_ref, ls1_ref, ls2_ref)
    for gi, (o_ref, l_ref) in enumerate(((o0_ref, l0_ref), (o1_ref, l1_ref), (o2_ref, l2_ref))):
        dil = DIL_PAIRS[gi][1]
        n = MERGE_ROWS // dil
        for r in range(dil):
            tok = pl.ds(r, n, stride=dil) if dil > 1 else slice(None)
            for i in range(2):
                lo = r * DIL_OUT_WIDTH + i * LANES
                ostage[gi][i][tok, :] = o_ref[:, lo:lo + LANES].astype(F32)
            lstage[gi][tok, :] = l_ref[:, r * LANES:(r + 1) * LANES]
    for r0 in range(0, MERGE_ROWS, MERGE_SLAB):
        rows = slice(r0, r0 + MERGE_SLAB)
        l0, l1, l2 = ls0_ref[rows, :], ls1_ref[rows, :], ls2_ref[rows, :]
        lmax = jnp.maximum(jnp.maximum(l0, l1), l2)
        e0, e1, e2 = jnp.exp(l0 - lmax), jnp.exp(l1 - lmax), jnp.exp(l2 - lmax)
        inv = 1.0 / (e0 + e1 + e2)
        ydil = jnp.zeros((MERGE_SLAB, DIL_OUT_WIDTH), F32)
        for gi, e in enumerate((e0, e1, e2)):
            og = jnp.concatenate([ostage[gi][0][rows, :], ostage[gi][1][rows, :]], axis=1)
            ydil = ydil + _dot_f32(e * inv, e4_ref[...]) * og
        merged = (_sigmoid(gate_ref[rows, 0:D_MODEL].astype(F32))
                  * _dot(yssd_ref[rows, :], wssd_ref[...])
                  + _sigmoid(gate_ref[rows, D_MODEL:2 * D_MODEL].astype(F32))
                  * _dot(ydil.astype(BF16), wdil_ref[...])
                  + _sigmoid(gate_ref[rows, 2 * D_MODEL:3 * D_MODEL].astype(F32))
                  * _dot(ymem_ref[rows, :], wmem_ref[...]))
        h_ref[rows, :] = x_ref[rows, :] + _dot(merged.astype(BF16), wout_ref[...])


def _merge(x2d, yssd, o, lse, ymem, gate, wssd, wdil, wmem, wout, e4):
    m, d = x2d.shape
    tm = MERGE_ROWS
    row = lambda width, dil=1: pl.BlockSpec((tm // dil, width * dil), lambda i: (i, 0))
    dils = [dil for _, dil in DIL_PAIRS]
    ng = len(dils)
    return pl.pallas_call(
        _merge_kernel,
        grid=(m // tm,),
        in_specs=[row(d), row(d)] + [row(DIL_OUT_WIDTH, dil) for dil in dils]
                 + [row(LANES, dil) for dil in dils]
                 + [row(MEM_HEADS * MEM_HEAD_PAD), row(3 * d),
                    _resident(wssd.shape), _resident(wdil.shape), _resident(wmem.shape),
                    _resident(wout.shape), _resident(e4.shape)],
        out_specs=row(d),
        out_shape=jax.ShapeDtypeStruct((m, d), F32),
        scratch_shapes=[pltpu.VMEM((tm, LANES), F32)] * (3 * ng),
        compiler_params=pltpu.CompilerParams(vmem_limit_bytes=VMEM_LIMIT_BYTES),
        name="merge",
    )(x2d, yssd, o[0], o[1], o[2], lse[0], lse[1], lse[2], ymem, gate, wssd, wdil, wmem, wout, e4)


def _pad_heads(w, axis):
    shape = list(w.shape)
    shape[axis:axis + 1] = [MEM_HEADS, MEM_HEAD_DIM]
    w = w.reshape(shape)
    pad = [(0, 0)] * w.ndim
    pad[axis + 1] = (0, MEM_HEAD_PAD - MEM_HEAD_DIM)
    w = jnp.pad(w, pad)
    shape[axis:axis + 2] = [MEM_HEADS * MEM_HEAD_PAD]
    return w.reshape(shape)


def _mix_block(x, mem, positions, norm_mix_w, w_in, ssd_conv_w, ssd_conv_b, ssd_dt_bias,
               ssd_a_log, ssd_d, ssd_norm_w, dil_q_norm_w, dil_k_norm_w, mem_norm_w, w_mem_kv,
               mem_q_norm_w, mem_k_norm_w, w_ssd_br, w_dil_br, w_mem_br, w_out):
    b, s, d = x.shape
    m = b * s
    pad_lanes = lambda v: jnp.pad(v, (0, LANES - v.shape[0])).reshape(1, LANES)
    w_packed = jnp.concatenate([
        w_in[:, OFF_Z:OFF_DT],
        jnp.pad(w_in[:, OFF_DT:OFF_DQ], ((0, 0), (0, LANES - SSD_HEADS))),
        w_in[:, OFF_DQ:OFF_MQ],
        _pad_heads(w_in[:, OFF_MQ:OFF_GATE], 1),
        w_in[:, OFF_GATE:]], axis=1).astype(BF16)
    half = ROPE_DIMS // 2
    inv = jnp.exp(-math.log(ROPE_THETA) * (2.0 / ROPE_DIMS) * jnp.arange(half, dtype=F32))
    invp = jnp.tile(inv, LANES // half).reshape(1, LANES)
    pos_b = jnp.broadcast_to(positions.astype(F32).reshape(m, 1), (m, LANES))
    tile2 = lambda v: jnp.tile(v, 2).reshape(1, LANES)
    mqnw = jnp.pad(mem_q_norm_w, (0, MEM_HEAD_PAD - MEM_HEAD_DIM)).reshape(1, MEM_HEAD_PAD)
    mknw = jnp.pad(mem_k_norm_w, (0, MEM_HEAD_PAD - MEM_HEAD_DIM)).reshape(1, MEM_HEAD_PAD)
    z, xbc, dt_raw, q0, q1, q2, k0, k1, k2, v0, v1, v2, mq, gate = _in_proj(
        x.reshape(m, d), pos_b, norm_mix_w.reshape(1, d), w_packed,
        tile2(dil_q_norm_w), tile2(dil_k_norm_w), mqnw, invp)

    expand = jnp.repeat(jnp.eye(LANES, SSD_HEADS, dtype=BF16), SSD_HEAD_DIM, axis=1)
    y_ssd = _ssd(xbc.reshape(b, s, -1), z.reshape(b, s, -1), dt_raw.reshape(b, s, -1),
                 ssd_conv_w, ssd_conv_b.reshape(1, -1), pad_lanes(ssd_dt_bias), pad_lanes(ssd_a_log),
                 jnp.repeat(ssd_d, SSD_HEAD_DIM).reshape(1, d), ssd_norm_w.reshape(1, d), expand)

    outs, lses = [], []
    for gi, (window, dilation) in enumerate(DIL_PAIRS):
        assert window // dilation == DIL_BLOCK
        per_batch = lambda t: t.reshape(b, s // dilation, -1)
        o, lse = _dil_attn(per_batch((q0, q1, q2)[gi]), per_batch((k0, k1, k2)[gi]),
                           per_batch((v0, v1, v2)[gi]), gi, dilation)
        outs.append(o.reshape(m // dilation, -1))
        lses.append(lse.reshape(m // dilation, -1))

    w_kv = jnp.concatenate([_pad_heads(w_mem_kv[:, :MEM_WIDTH], 1),
                            _pad_heads(w_mem_kv[:, MEM_WIDTH:], 1)], axis=1).astype(BF16)
    mk, mv = _mem_kv(mem, mem_norm_w.reshape(1, d), w_kv, mknw)
    y_mem = _mem_attn(mq.reshape(b, s, -1), mk, mv).reshape(m, -1)

    e4 = jnp.repeat(jnp.eye(LANES, DIL_OUT_WIDTH // DIL_HEAD_DIM, dtype=F32), DIL_HEAD_DIM, axis=1)
    return _merge(x.reshape(m, d), y_ssd.reshape(m, d), outs, lses, y_mem, gate,
                  w_ssd_br.astype(BF16), w_dil_br.astype(BF16),
                  _pad_heads(w_mem_br, 0).astype(BF16), w_out.astype(BF16), e4)


PEER_TOKENS = 512
PEER_A_PER_STEP = 8
ROUTE_LANES = 128


def _oddeven_merge(lo, hi, r):
    step = r * 2
    if step < hi - lo:
        yield from _oddeven_merge(lo, hi, step)
        yield from _oddeven_merge(lo + r, hi, step)
        yield from [(i, i + r) for i in range(lo + r, hi - r, step)]
    else:
        yield (lo, lo + r)


def _oddeven_merge_sort(lo, hi):
    if hi - lo >= 1:
        mid = lo + (hi - lo) // 2
        yield from _oddeven_merge_sort(lo, mid)
        yield from _oddeven_merge_sort(mid + 1, hi)
        yield from _oddeven_merge(lo, hi, 1)


_SORT16 = tuple(_oddeven_merge_sort(0, PEER_TOPK - 1))


def _cmpx(a, i, j):
    hi = jnp.maximum(a[i], a[j])
    lo = jnp.minimum(a[i], a[j])
    a[i], a[j] = hi, lo


def _sort16_desc(a):
    for i, j in _SORT16:
        _cmpx(a, i, j)
    return a


def _bitonic16_desc(a):
    for d in (8, 4, 2, 1):
        for i in range(PEER_TOPK):
            if not i & d:
                _cmpx(a, i, i + d)
    return a


def _merge_top16(a, b):
    c = [jnp.maximum(a[k], b[PEER_TOPK - 1 - k]) for k in range(PEER_TOPK)]
    return _bitonic16_desc(c)


def _merge_sublanes(a):
    for sh in (4, 2, 1):
        b = [pltpu.roll(x, sh, axis=0) for x in a]
        a = _merge_top16(a, b)
    return a


def _count_prefix(rows, test):
    def pick(bits, lo):
        if not bits:
            return rows[lo]
        (c, stride), rest = bits[0], bits[1:]
        return jnp.where(c, pick(rest, lo + stride), pick(rest, lo))

    bits = []
    for stride in (8, 4, 2, 1):
        bits.append((test(pick(bits, stride - 1)), stride))
    count = None
    for c, stride in bits:
        term = jnp.where(c, float(stride), 0.0)
        count = term if count is None else count + term
    return jnp.where(test(rows[PEER_TOPK - 1]), float(PEER_TOPK), count)


def _top16_rows(s):
    rows = [s[8 * i:8 * i + 8, :] for i in range(PEER_N_KEYS // 8)]
    return _merge_sublanes(_sort16_desc(rows))


def _peer_route_kernel(h_ref, nw_ref, wqt_ref, keys_ref,
                       hnt_ref, cnt_ref, f1_ref, rank_ref, e2_ref, s1_ref, s2_ref):
    t = h_ref.shape[0]
    x = h_ref[...]
    ms = jnp.mean(x * x, axis=-1, keepdims=True)
    hn = x * lax.rsqrt(ms + EPS) * nw_ref[...]
    hnt = hn.T.astype(BF16)
    hnt_ref[...] = hnt
    ch = ROUTE_LANES
    sub = lax.broadcasted_iota(jnp.int32, (8, ch), 0)

    def distribute(vs):
        out = vs[7]
        for r in range(6, -1, -1):
            out = jnp.where(sub == r, vs[r], out)
        return out

    for h in range(PEER_HEADS):
        qt = jnp.dot(wqt_ref[h * PEER_QUERY_DIM:(h + 1) * PEER_QUERY_DIM, :], hnt,
                     preferred_element_type=F32).astype(BF16)
        half = PEER_QUERY_DIM // 2
        s1_ref[...] = jnp.dot(keys_ref[h, 0], qt[:half], preferred_element_type=F32)
        s2_ref[...] = jnp.dot(keys_ref[h, 1], qt[half:], preferred_element_type=F32)
        for c0 in range(0, t, ch):
            cols = slice(c0, c0 + ch)
            s1 = s1_ref[:, cols]
            s2 = s2_ref[:, cols]
            v1 = _top16_rows(s1)
            v2 = _top16_rows(s2)
            a_lo = distribute(v1[:8])
            a_hi = distribute(v1[8:])
            lo = [a_lo + v2[j] for j in range(PEER_TOPK)]
            hi = [a_hi + v2[j] for j in range(PEER_TOPK)]
            tau = _merge_sublanes(_merge_top16(lo, hi))[PEER_TOPK - 1]
            e_lo = jnp.exp(a_lo - v1[0])
            e_hi = jnp.exp(a_hi - v1[0])
            zp = jnp.zeros((8, ch), F32)
            for j in range(PEER_TOPK):
                e2j = jnp.exp(v2[j] - v2[0])
                zp = zp + jnp.where(lo[j] >= tau, e_lo * e2j, 0.0)
                zp = zp + jnp.where(hi[j] >= tau, e_hi * e2j, 0.0)
            z = jnp.sum(zp, axis=0, keepdims=True)
            tau_r = tau[0:1, :]
            v2r = [v[0:1, :] for v in v2]
            cnt = _count_prefix(v2r, lambda r: s1 + r >= tau_r)
            rank = _count_prefix(v2r, lambda r: r > s2)
            cnt_ref[h, :, cols] = cnt
            f1_ref[h, :, cols] = jnp.exp(s1 - v1[0][0:1, :]) * (1.0 / z)
            rank_ref[h, :, cols] = rank.astype(BF16)
            e2_ref[h, :, cols] = jnp.exp(s2 - v2[0][0:1, :]).astype(BF16)


def _peer_route(h2d, norm_w, wqt, keys):
    n, d = h2d.shape
    t = PEER_TOKENS
    route = lambda dt: jax.ShapeDtypeStruct((PEER_HEADS, PEER_N_KEYS, n), dt)
    route_spec = pl.BlockSpec((PEER_HEADS, PEER_N_KEYS, t), lambda i: (0, 0, i))
    return pl.pallas_call(
        _peer_route_kernel,
        grid=(n // t,),
        in_specs=[
            pl.BlockSpec((t, d), lambda i: (i, 0)),
            pl.BlockSpec((1, d), lambda i: (0, 0)),
            pl.BlockSpec(wqt.shape, lambda i: (0, 0)),
            pl.BlockSpec(keys.shape, lambda i: (0, 0, 0, 0)),
        ],
        out_specs=[pl.BlockSpec((d, t), lambda i: (0, i)),
                   route_spec, route_spec, route_spec, route_spec],
        out_shape=[jax.ShapeDtypeStruct((d, n), BF16), route(F32), route(F32), route(BF16), route(BF16)],
        scratch_shapes=[pltpu.VMEM((PEER_N_KEYS, t), F32), pltpu.VMEM((PEER_N_KEYS, t), F32)],
        compiler_params=pltpu.CompilerParams(vmem_limit_bytes=VMEM_LIMIT_BYTES),
        name="peer_route",
    )(h2d, norm_w.reshape(1, d), wqt, keys)


BF16_ROWS = 16


def _peer_expert_kernel(h_ref, hnt_ref, cnt_ref, f1_ref, rank_ref, e2_ref,
                        down_ref, upt_ref, upt_last_ref, o_ref, acc_ref, coef_ref, w_ref,
                        cntb_ref, f1b_ref):
    step = pl.program_id(1)
    n_blocks = pl.num_programs(1)
    t = hnt_ref.shape[1]
    cur = step % 2

    @pl.when(step == 0)
    def _():
        acc_ref[...] = jnp.zeros_like(acc_ref)
        coef_ref[1] = jnp.zeros_like(coef_ref[1])

    for q in range(PEER_A_PER_STEP):
        a = step * PEER_A_PER_STEP + q
        for h in range(PEER_HEADS):
            i = q * PEER_HEADS + h
            cntb_ref[i] = jnp.broadcast_to(cnt_ref[h, pl.ds(a, 1), :], (BF16_ROWS, t)).astype(BF16)
            f1b_ref[i] = jnp.broadcast_to(f1_ref[h, pl.ds(a, 1), :], (BF16_ROWS, t)).astype(BF16)
    for c0 in range(0, t, 2 * LANES):
        cols = slice(c0, c0 + 2 * LANES)
        for r0 in range(0, PEER_N_KEYS, BF16_ROWS):
            rows = slice(r0, r0 + BF16_ROWS)
            ranks = [rank_ref[h, rows, cols] for h in range(PEER_HEADS)]
            e2s = [e2_ref[h, rows, cols] for h in range(PEER_HEADS)]
            for q in range(PEER_A_PER_STEP):
                w = None
                for h in range(PEER_HEADS):
                    i = q * PEER_HEADS + h
                    wh = jnp.where(ranks[h] < cntb_ref[i, :, cols], e2s[h],
                                   jnp.zeros_like(e2s[h])) * f1b_ref[i, :, cols]
                    w = wh if w is None else w + wh
                er = slice(q * PEER_N_KEYS + r0, q * PEER_N_KEYS + r0 + BF16_ROWS)
                w_ref[er, cols] = w

    acc_ref[...] += jnp.dot(upt_ref[...], coef_ref[1 - cur], preferred_element_type=F32)
    s = jnp.dot(down_ref[...], hnt_ref[...], preferred_element_type=F32)
    sb = s.astype(BF16)
    act = (0.5 * sb) * (1.0 + lax.erf(sb * (2.0 ** -0.5)))
    coef_ref[cur] = w_ref[...] * act

    @pl.when(step == n_blocks - 1)
    def _():
        last = jnp.dot(upt_last_ref[...], coef_ref[cur], preferred_element_type=F32)
        o_ref[...] = h_ref[...] + (acc_ref[...] + last).T


def _peer_expert(h2d, hnt, cnt, f1, rank, e2, down_bf, upt_bf):
    n, d = h2d.shape
    t = PEER_TOKENS
    eb = PEER_A_PER_STEP * PEER_N_KEYS
    pairs = PEER_A_PER_STEP * PEER_HEADS
    route_spec = pl.BlockSpec((PEER_HEADS, PEER_N_KEYS, t), lambda i, j: (0, 0, i))
    n_blocks = PEER_N_KEYS // PEER_A_PER_STEP
    return pl.pallas_call(
        _peer_expert_kernel,
        grid=(n // t, n_blocks),
        in_specs=[
            pl.BlockSpec((t, d), lambda i, j: (i, 0)),
            pl.BlockSpec((d, t), lambda i, j: (0, i)),
            route_spec, route_spec, route_spec, route_spec,
            pl.BlockSpec((eb, d), lambda i, j: (j, 0)),
            pl.BlockSpec((d, eb), lambda i, j: (0, jnp.maximum(j - 1, 0))),
            pl.BlockSpec((d, eb), lambda i, j: (0, n_blocks - 1)),
        ],
        out_specs=pl.BlockSpec((t, d), lambda i, j: (i, 0)),
        out_shape=jax.ShapeDtypeStruct((n, d), F32),
        scratch_shapes=[pltpu.VMEM((d, t), F32), pltpu.VMEM((2, eb, t), BF16), pltpu.VMEM((eb, t), BF16),
                        pltpu.VMEM((pairs, BF16_ROWS, t), BF16), pltpu.VMEM((pairs, BF16_ROWS, t), BF16)],
        compiler_params=pltpu.CompilerParams(
            dimension_semantics=("arbitrary", "arbitrary"),
            vmem_limit_bytes=VMEM_LIMIT_BYTES),
        name="peer_expert",
    )(h2d, hnt, cnt, f1, rank, e2, down_bf, upt_bf, upt_bf)


def _peer_block(h2d, norm_w, w_query, sub_keys, expert_down, expert_up):
    wqt = w_query.T.astype(BF16)
    hnt, cnt, f1, rank, e2 = _peer_route(h2d, norm_w, wqt, sub_keys.astype(BF16))
    return _peer_expert(h2d, hnt, cnt, f1, rank, e2,
                        expert_down.astype(BF16), expert_up.T.astype(BF16))


def kernel(x, mem, positions, norm_mix_w, w_in, ssd_conv_w, ssd_conv_b, ssd_dt_bias,
           ssd_a_log, ssd_d, ssd_norm_w, dil_q_norm_w, dil_k_norm_w, mem_norm_w, w_mem_kv,
           mem_q_norm_w, mem_k_norm_w, w_ssd_br, w_dil_br, w_mem_br, w_out, norm_ffn_w,
           peer_w_query, peer_sub_keys, peer_down, peer_up):
    b, s, d = x.shape
    assert norm_mix_w.shape[0] == 1, "single-layer block"
    h = _mix_block(x, mem, positions, norm_mix_w[0], w_in[0], ssd_conv_w[0], ssd_conv_b[0],
                   ssd_dt_bias[0], ssd_a_log[0], ssd_d[0], ssd_norm_w[0], dil_q_norm_w[0],
                   dil_k_norm_w[0], mem_norm_w[0], w_mem_kv[0], mem_q_norm_w[0], mem_k_norm_w[0],
                   w_ssd_br[0], w_dil_br[0], w_mem_br[0], w_out[0])
    h = _peer_block(h, norm_ffn_w[0], peer_w_query[0], peer_sub_keys[0], peer_down[0], peer_up[0])
    return h.reshape(b, s, d)
```

```python
import math

import jax
import jax.numpy as jnp
from jax import lax
from jax.experimental import pallas as pl
from jax.experimental.pallas import tpu as pltpu

D_MODEL = 1024
EPS = 1e-6
SSD_HEAD_DIM = 64
SSD_HEADS = 16
SSD_GROUPS = 4
SSD_HEADS_PER_GROUP = 4
SSD_STATE = 128
SSD_CONV = 4
SSD_CHUNK = 128
SSD_CONV_DIM = 2048
DIL_PAIRS = ((128, 1), (512, 4), (2048, 16))
DIL_HEAD_DIM = 64
DIL_WIDTH = 768
DIL_OUT_WIDTH = 256
MEM_HEADS = 4
MEM_HEAD_DIM = 192
MEM_HEAD_PAD = 256
MEM_WIDTH = 768
ROPE_THETA = 500000.0
ROPE_DIMS = 16
PEER_HEADS = 8
PEER_N_KEYS = 128
PEER_QUERY_DIM = 256
PEER_TOPK = 16
OFF_Z = 0
OFF_XBC = 1024
OFF_DT = 3072
OFF_DQ = 3088
OFF_DK = OFF_DQ + DIL_WIDTH
OFF_DV = OFF_DK + DIL_WIDTH
OFF_MQ = OFF_DV + DIL_WIDTH
OFF_GATE = OFF_MQ + MEM_WIDTH
F32 = jnp.float32
BF16 = jnp.bfloat16
HIGHEST = lax.Precision.HIGHEST
LANES = 128
VMEM_LIMIT_BYTES = 56 * 1024 * 1024

P_Z = 0
P_XBC = 1024
P_DT = 3072
P_Q = 3200
P_K = P_Q + DIL_WIDTH
P_V = P_K + DIL_WIDTH
P_MQ = P_V + DIL_WIDTH
P_GATE = P_MQ + MEM_HEADS * MEM_HEAD_PAD
P_WIDTH = P_GATE + 3 * D_MODEL


def _resident(shape):
    nd = len(shape)
    return pl.BlockSpec(shape, lambda *_: (0,) * nd, pipeline_mode=pl.Buffered(1))


def _sigmoid(x):
    return 1.0 / (1.0 + jnp.exp(-x))


def _dot(a, b):
    return jnp.dot(a, b, preferred_element_type=F32)


def _dot_nt(a, b):
    return lax.dot_general(a, b, (((1,), (1,)), ((), ())), preferred_element_type=F32)


def _dot_f32(a, b):
    return jnp.dot(a, b, precision=HIGHEST, preferred_element_type=F32)


IN_PROJ_ROWS = 512
PLAIN_CHUNK = 256


def _in_proj_kernel(x_ref, pos_ref, nw_ref, w_ref, qnw_ref, knw_ref, mqnw_ref, invp_ref,
                    z_ref, xbc_ref, dt_ref, q0_ref, q1_ref, q2_ref, k0_ref, k1_ref, k2_ref,
                    v0_ref, v1_ref, v2_ref, mq_ref, gate_ref, u_ref, stage_a_ref, stage_b_ref):
    x = x_ref[...]
    ms = jnp.mean(x * x, axis=-1, keepdims=True)
    u_ref[...] = (x * lax.rsqrt(ms + EPS) * nw_ref[...]).astype(BF16)

    def seg(off, width):
        return _dot(u_ref[...], w_ref[:, off:off + width])

    for j in range(0, 1024, PLAIN_CHUNK):
        z_ref[:, j:j + PLAIN_CHUNK] = seg(P_Z + j, PLAIN_CHUNK).astype(BF16)
    for j in range(0, SSD_CONV_DIM, PLAIN_CHUNK):
        xbc_ref[:, j:j + PLAIN_CHUNK] = seg(P_XBC + j, PLAIN_CHUNK).astype(BF16)
    dt_ref[...] = seg(P_DT, LANES)
    rows = x.shape[0]

    stages = (stage_a_ref, stage_b_ref)

    def put_group(gi, o_ref):
        dil = DIL_PAIRS[gi][1]
        for i, st in enumerate(stages):
            for r in range(dil):
                tok = pl.ds(r, rows // dil, stride=dil) if dil > 1 else slice(None)
                lo = r * DIL_OUT_WIDTH + i * LANES
                o_ref[:, lo:lo + LANES] = st[tok, :].astype(BF16)

    for gi, o_ref in enumerate((v0_ref, v1_ref, v2_ref)):
        vt = seg(P_V + gi * DIL_OUT_WIDTH, DIL_OUT_WIDTH)
        for i, st in enumerate(stages):
            st[...] = vt[:, i * LANES:(i + 1) * LANES]
        put_group(gi, o_ref)
    for j in range(0, 3 * D_MODEL, PLAIN_CHUNK):
        gate_ref[:, j:j + PLAIN_CHUNK] = seg(P_GATE + j, PLAIN_CHUNK).astype(BF16)

    ang = pos_ref[...] * invp_ref[...]
    cos = jnp.cos(ang)
    sin = jnp.sin(ang)
    lane = lax.broadcasted_iota(jnp.int32, (rows, LANES), 1)
    l64 = lane & (DIL_HEAD_DIM - 1)
    first = l64 < ROPE_DIMS // 2
    rotated = l64 < ROPE_DIMS
    cm = jnp.where(rotated, cos, 1.0)
    sm = jnp.where(rotated, jnp.where(first, -sin, sin), 0.0)
    left = lane < DIL_HEAD_DIM

    def qk_norm_rope(off, w_row_ref, o_refs):
        for gi, o_ref in enumerate(o_refs):
            t2 = seg(off + gi * DIL_OUT_WIDTH, DIL_OUT_WIDTH)
            for i in range(2):
                t = t2[:, i * LANES:(i + 1) * LANES]
                sq = t * t
                ssl = jnp.sum(jnp.where(left, sq, 0.0), axis=-1, keepdims=True)
                ssr = jnp.sum(jnp.where(left, 0.0, sq), axis=-1, keepdims=True)
                r = jnp.where(left, lax.rsqrt(ssl / DIL_HEAD_DIM + EPS),
                              lax.rsqrt(ssr / DIL_HEAD_DIM + EPS))
                tn = t * r * w_row_ref[...]
                rot = jnp.where(first, pltpu.roll(tn, LANES - ROPE_DIMS // 2, axis=1),
                                pltpu.roll(tn, ROPE_DIMS // 2, axis=1))
                stages[i][...] = tn * cm + rot * sm
            put_group(gi, o_ref)

    qk_norm_rope(P_Q, qnw_ref, (q0_ref, q1_ref, q2_ref))
    qk_norm_rope(P_K, knw_ref, (k0_ref, k1_ref, k2_ref))

    for h in range(MEM_HEADS):
        t2 = seg(P_MQ + h * MEM_HEAD_PAD, MEM_HEAD_PAD)
        ss = jnp.sum(t2 * t2, axis=-1, keepdims=True)
        mq_ref[:, h * MEM_HEAD_PAD:(h + 1) * MEM_HEAD_PAD] = (
            t2 * lax.rsqrt(ss / MEM_HEAD_DIM + EPS) * mqnw_ref[...]).astype(BF16)


def _in_proj(x2d, pos_b, norm_w, w_packed, qnw, knw, mqnw, invp):
    m, d = x2d.shape
    tm = IN_PROJ_ROWS
    row = lambda width, dil=1: pl.BlockSpec((tm // dil, width * dil), lambda i: (i, 0))
    dil_outs = [(DIL_OUT_WIDTH, BF16, dil) for _, dil in DIL_PAIRS]
    outs = ([(1024, BF16, 1), (SSD_CONV_DIM, BF16, 1), (LANES, F32, 1)] + dil_outs * 3
            + [(MEM_HEADS * MEM_HEAD_PAD, BF16, 1), (3 * D_MODEL, BF16, 1)])
    return pl.pallas_call(
        _in_proj_kernel,
        grid=(m // tm,),
        in_specs=[row(d), row(LANES), _resident((1, d)), _resident(w_packed.shape),
                  _resident((1, LANES)), _resident((1, LANES)), _resident((1, MEM_HEAD_PAD)),
                  _resident((1, LANES))],
        out_specs=[row(w, dil) for w, _, dil in outs],
        out_shape=[jax.ShapeDtypeStruct((m // dil, w * dil), dt) for w, dt, dil in outs],
        scratch_shapes=[pltpu.VMEM((tm, d), BF16), pltpu.VMEM((tm, LANES), F32),
                        pltpu.VMEM((tm, LANES), F32)],
        compiler_params=pltpu.CompilerParams(vmem_limit_bytes=VMEM_LIMIT_BYTES),
        name="in_proj",
    )(x2d, pos_b, norm_w, w_packed, qnw, knw, mqnw, invp)


def _ssd_kernel(xbc_ref, z_ref, dt_ref, cw_ref, cb_ref, dtb_ref, alog_ref, dfull_ref, nw_ref,
                e_ref, y_ref, tail_ref, state_ref):
    c = pl.program_id(1)
    L = SSD_CHUNK

    @pl.when(c == 0)
    def _():
        tail_ref[...] = jnp.zeros_like(tail_ref)
        state_ref[...] = jnp.zeros_like(state_ref)

    cur = xbc_ref[0].astype(F32)
    xp = jnp.concatenate([tail_ref[...], cur], axis=0)
    conv = cb_ref[...] + cw_ref[SSD_CONV - 1:SSD_CONV, :] * cur
    for k in range(SSD_CONV - 1):
        lo = 8 - (SSD_CONV - 1) + k
        conv = conv + cw_ref[k:k + 1, :] * xp[lo:lo + L]
    tail_ref[...] = cur[L - 8:L]
    xa = conv * _sigmoid(conv)
    xs = xa[:, :D_MODEL]
    gn = SSD_GROUPS * SSD_STATE
    bm = xa[:, D_MODEL:D_MODEL + gn].astype(BF16)
    cm = xa[:, D_MODEL + gn:].astype(BF16)

    pre = dt_ref[0] + dtb_ref[...]
    dt = jnp.maximum(pre, 0.0) + jnp.log(1.0 + jnp.exp(-jnp.abs(pre)))
    da = dt * (-jnp.exp(alog_ref[...]))
    ri = lax.broadcasted_iota(jnp.int32, (L, L), 0)
    ci = lax.broadcasted_iota(jnp.int32, (L, L), 1)
    tril = ri >= ci
    acum = _dot_f32(tril.astype(F32), da)
    acum_t = acum.T
    def expand(v):
        hi = v.astype(BF16)
        lo = (v - hi.astype(F32)).astype(BF16)
        return _dot(hi, e_ref[...]) + _dot(lo, e_ref[...])

    dt_full = expand(dt)
    ea_full = expand(jnp.exp(acum))
    dec_full = expand(jnp.exp(acum[L - 1:L, :] - acum))
    xdt = xs * dt_full
    xdt_b = xdt.astype(BF16)
    xdd_b = (xdt * dec_full).astype(BF16)
    gw = SSD_HEADS_PER_GROUP * SSD_HEAD_DIM
    head_of_lane = lax.broadcasted_iota(jnp.int32, (L, gw), 1) // SSD_HEAD_DIM
    ys = []
    for g in range(SSD_GROUPS):
        bg = bm[:, g * SSD_STATE:(g + 1) * SSD_STATE]
        cg = cm[:, g * SSD_STATE:(g + 1) * SSD_STATE]
        gs = slice(g * gw, (g + 1) * gw)
        cb = _dot_nt(cg, bg)
        st_old = state_ref[g]
        yg = _dot(cg, st_old.astype(BF16)) * ea_full[:, gs]
        ms = []
        for r in range(SSD_HEADS_PER_GROUP):
            h = g * SSD_HEADS_PER_GROUP + r
            seg = acum[:, h:h + 1] - acum_t[h:h + 1, :]
            lmat = jnp.exp(jnp.where(tril, seg, -jnp.inf))
            ms.append((cb * lmat).astype(BF16))
        yd = _dot(jnp.concatenate(ms, axis=0), xdt_b[:, gs])
        for r in range(SSD_HEADS_PER_GROUP):
            yg = yg + jnp.where(head_of_lane == r, yd[r * L:(r + 1) * L], 0.0)
        ys.append(yg)
        s_new = lax.dot_general(bg, xdd_b[:, gs], (((0,), (0,)), ((), ())),
                                preferred_element_type=F32)
        state_ref[g] = st_old * ea_full[L - 1:L, gs] + s_new
    y = jnp.concatenate(ys, axis=1) + dfull_ref[...] * xs
    zf = z_ref[0].astype(F32)
    y = y * (zf * _sigmoid(zf))
    outs = []
    for g in range(SSD_GROUPS):
        yg = y[:, g * gw:(g + 1) * gw]
        ss = jnp.sum(yg * yg, axis=-1, keepdims=True)
        outs.append(yg * lax.rsqrt(ss / gw + EPS))
    y_ref[0] = (jnp.concatenate(outs, axis=1) * nw_ref[...]).astype(BF16)


def _ssd(xbc, z, dt_raw, conv_w, conv_b, dt_bias, a_log, d_full, norm_w, expand):
    b, s, _ = xbc.shape
    L = SSD_CHUNK
    blk = lambda width: pl.BlockSpec((1, L, width), lambda i, c: (i, c, 0))
    return pl.pallas_call(
        _ssd_kernel,
        grid=(b, s // L),
        in_specs=[blk(SSD_CONV_DIM), blk(D_MODEL), blk(LANES),
                  _resident(conv_w.shape), _resident(conv_b.shape), _resident(dt_bias.shape),
                  _resident(a_log.shape), _resident(d_full.shape), _resident(norm_w.shape),
                  _resident(expand.shape)],
        out_specs=blk(D_MODEL),
        out_shape=jax.ShapeDtypeStruct((b, s, D_MODEL), BF16),
        scratch_shapes=[pltpu.VMEM((8, SSD_CONV_DIM), F32),
                        pltpu.VMEM((SSD_GROUPS, SSD_STATE, SSD_HEADS_PER_GROUP * SSD_HEAD_DIM), F32)],
        compiler_params=pltpu.CompilerParams(
            dimension_semantics=("arbitrary", "arbitrary"), vmem_limit_bytes=VMEM_LIMIT_BYTES),
        name="ssd",
    )(xbc, z, dt_raw, conv_w, conv_b, dt_bias, a_log, d_full, norm_w, expand)


DIL_BLOCK = 128
DIL_QROWS = 512


def _dil_attn_kernel(q_ref, kc_ref, kp_ref, vc_ref, vp_ref, o_ref, lse_ref):
    n = pl.program_id(2)
    blk = DIL_BLOCK
    w = DIL_OUT_WIDTH
    nh = w // DIL_HEAD_DIM
    head_of_lane = lax.broadcasted_iota(jnp.int32, (blk, w), 1) // DIL_HEAD_DIM
    lane = lax.broadcasted_iota(jnp.int32, (blk, LANES), 1)
    qi = lax.broadcasted_iota(jnp.int32, (nh * blk, 2 * blk), 0) % blk
    ki = lax.broadcasted_iota(jnp.int32, (nh * blk, 2 * blk), 1)
    band = (ki >= qi) & (ki <= qi + blk)
    first_valid = band & (ki + jnp.minimum(n, 1) * blk >= blk)
    for sb in range(DIL_QROWS // blk):
        rows = slice(sb * blk, (sb + 1) * blk)
        q = q_ref[0, rows, :]
        if sb == 0:
            kprev, vprev, valid = kp_ref[0], vp_ref[0], first_valid
        else:
            prev = slice((sb - 1) * blk, sb * blk)
            kprev, vprev, valid = kc_ref[0, prev, :], vc_ref[0, prev, :], band
        kk = jnp.concatenate([kprev, kc_ref[0, rows, :]], axis=0)
        vv = jnp.concatenate([vprev, vc_ref[0, rows, :]], axis=0)
        qs = jnp.concatenate([jnp.where(head_of_lane == h, q, jnp.zeros_like(q))
                              for h in range(nh)], axis=0)
        s = _dot_nt(qs, kk) * (DIL_HEAD_DIM ** -0.5)
        s = jnp.where(valid, s, -jnp.inf)
        m = jnp.max(s, axis=-1, keepdims=True)
        p = jnp.exp(s - m)
        l = jnp.sum(p, axis=-1, keepdims=True)
        on = _dot(p.astype(BF16), vv) / l
        lse = m + jnp.log(l)
        o_acc = jnp.zeros((blk, w), F32)
        lse_t = jnp.zeros((blk, LANES), F32)
        for h in range(nh):
            hr = slice(h * blk, (h + 1) * blk)
            o_acc = jnp.where(head_of_lane == h, on[hr], o_acc)
            lse_t = jnp.where(lane == h, lse[hr], lse_t)
        o_ref[0, rows, :] = o_acc.astype(BF16)
        lse_ref[0, rows, :] = lse_t


def _dil_attn(q3, k3, v3, gi, dilation):
    b, m, _ = q3.shape
    qr = DIL_QROWS
    sub = qr // DIL_BLOCK
    cur = pl.BlockSpec((1, qr, DIL_OUT_WIDTH), lambda i, r, n: (i, n, r))
    prev = pl.BlockSpec((1, DIL_BLOCK, DIL_OUT_WIDTH),
                        lambda i, r, n: (i, jnp.maximum(n * sub - 1, 0), r))
    return pl.pallas_call(
        _dil_attn_kernel,
        grid=(b, dilation, m // qr),
        in_specs=[cur, cur, prev, cur, prev],
        out_specs=[pl.BlockSpec((1, qr, DIL_OUT_WIDTH), lambda i, r, n: (i, n, r)),
                   pl.BlockSpec((1, qr, LANES), lambda i, r, n: (i, n, r))],
        out_shape=[jax.ShapeDtypeStruct((b, m, dilation * DIL_OUT_WIDTH), BF16),
                   jax.ShapeDtypeStruct((b, m, dilation * LANES), F32)],
        compiler_params=pltpu.CompilerParams(vmem_limit_bytes=VMEM_LIMIT_BYTES),
        name=f"dil_attn_g{gi}",
    )(q3, k3, k3, v3, v3)


def _mem_kv_kernel(mem_ref, nw_ref, w_ref, knw_ref, k_ref, v_ref):
    x = mem_ref[0]
    ms = jnp.mean(x * x, axis=-1, keepdims=True)
    u = (x * lax.rsqrt(ms + EPS) * nw_ref[...]).astype(BF16)
    width = MEM_HEADS * MEM_HEAD_PAD
    for h in range(MEM_HEADS):
        hs = slice(h * MEM_HEAD_PAD, (h + 1) * MEM_HEAD_PAD)
        kh = _dot(u, w_ref[:, hs])
        ss = jnp.sum(kh * kh, axis=-1, keepdims=True)
        k_ref[0, :, hs] = (kh * lax.rsqrt(ss / MEM_HEAD_DIM + EPS) * knw_ref[...]).astype(BF16)
        vs = slice(width + h * MEM_HEAD_PAD, width + (h + 1) * MEM_HEAD_PAD)
        v_ref[0, :, hs] = _dot(u, w_ref[:, vs]).astype(BF16)


def _mem_kv(mem, norm_w, w_kv, knw):
    b, n_mem, d = mem.shape
    width = MEM_HEADS * MEM_HEAD_PAD
    blk = pl.BlockSpec((1, n_mem, width), lambda i: (i, 0, 0))
    return pl.pallas_call(
        _mem_kv_kernel,
        grid=(b,),
        in_specs=[pl.BlockSpec((1, n_mem, d), lambda i: (i, 0, 0)), _resident((1, d)),
                  _resident(w_kv.shape), _resident((1, MEM_HEAD_PAD))],
        out_specs=[blk, blk],
        out_shape=[jax.ShapeDtypeStruct((b, n_mem, width), BF16)] * 2,
        compiler_params=pltpu.CompilerParams(vmem_limit_bytes=VMEM_LIMIT_BYTES),
        name="mem_kv",
    )(mem, norm_w, w_kv, knw)


MEM_QROWS = 512


def _mem_attn_kernel(q_ref, k_ref, v_ref, o_ref):
    for h in range(MEM_HEADS):
        hs = slice(h * MEM_HEAD_PAD, (h + 1) * MEM_HEAD_PAD)
        s = _dot_nt(q_ref[0, :, hs], k_ref[0, :, hs]) * (MEM_HEAD_DIM ** -0.5)
        m = jnp.max(s, axis=-1, keepdims=True)
        p = jnp.exp(s - m)
        l = jnp.sum(p, axis=-1, keepdims=True)
        o_ref[0, :, hs] = (_dot(p.astype(BF16), v_ref[0, :, hs]) / l).astype(BF16)


def _mem_attn(q, k, v):
    b, s, width = q.shape
    n_mem = k.shape[1]
    tm = MEM_QROWS
    kv = pl.BlockSpec((1, n_mem, width), lambda i, j: (i, 0, 0))
    return pl.pallas_call(
        _mem_attn_kernel,
        grid=(b, s // tm),
        in_specs=[pl.BlockSpec((1, tm, width), lambda i, j: (i, j, 0)), kv, kv],
        out_specs=pl.BlockSpec((1, tm, width), lambda i, j: (i, j, 0)),
        out_shape=jax.ShapeDtypeStruct((b, s, width), BF16),
        compiler_params=pltpu.CompilerParams(vmem_limit_bytes=VMEM_LIMIT_BYTES),
        name="mem_attn",
    )(q, k, v)


MERGE_ROWS = 512
MERGE_SLAB = 256


def _merge_kernel(x_ref, yssd_ref, o0_ref, o1_ref, o2_ref, l0_ref, l1_ref, l2_ref, ymem_ref,
                  gate_ref, wssd_ref, wdil_ref, wmem_ref, wout_ref, e4_ref, h_ref,
                  os0a_ref, os0b_ref, os1a_ref, os1b_ref, os2a_ref, os2b_ref,
                  ls0_ref, ls1_ref, ls2_ref):
    ostage = ((os0a_ref, os0b_ref), (os1a_ref, os1b_ref), (os2a_ref, os2b_ref))
    lstage = (ls0_ref, ls1_ref, ls2_ref)
    for gi, (o_ref, l_ref) in enumerate(((o0_ref, l0_ref), (o1_ref, l1_ref), (o2_ref, l2_ref))):
        dil = DIL_PAIRS[gi][1]
        n = MERGE_ROWS // dil
        for r in range(dil):
            tok = pl.ds(r, n, stride=dil) if dil > 1 else slice(None)
            for i in range(2):
                lo = r * DIL_OUT_WIDTH + i * LANES
                ostage[gi][i][tok, :] = o_ref[:, lo:lo + LANES].astype(F32)
            lstage[gi][tok, :] = l_ref[:, r * LANES:(r + 1) * LANES]
    for r0 in range(0, MERGE_ROWS, MERGE_SLAB):
        rows = slice(r0, r0 + MERGE_SLAB)
        l0, l1, l2 = ls0_ref[rows, :], ls1_ref[rows, :], ls2_ref[rows, :]
        lmax = jnp.maximum(jnp.maximum(l0, l1), l2)
        e0, e1, e2 = jnp.exp(l0 - lmax), jnp.exp(l1 - lmax), jnp.exp(l2 - lmax)
        inv = 1.0 / (e0 + e1 + e2)
        ydil = jnp.zeros((MERGE_SLAB, DIL_OUT_WIDTH), F32)
        for gi, e in enumerate((e0, e1, e2)):
            og = jnp.concatenate([ostage[gi][0][rows, :], ostage[gi][1][rows, :]], axis=1)
            ydil = ydil + _dot_f32(e * inv, e4_ref[...]) * og
        merged = (_sigmoid(gate_ref[rows, 0:D_MODEL].astype(F32))
                  * _dot(yssd_ref[rows, :], wssd_ref[...])
                  + _sigmoid(gate_ref[rows, D_MODEL:2 * D_MODEL].astype(F32))
                  * _dot(ydil.astype(BF16), wdil_ref[...])
                  + _sigmoid(gate_ref[rows, 2 * D_MODEL:3 * D_MODEL].astype(F32))
                  * _dot(ymem_ref[rows, :], wmem_ref[...]))
        h_ref[rows, :] = x_ref[rows, :] + _dot(merged.astype(BF16), wout_ref[...])


def _merge(x2d, yssd, o, lse, ymem, gate, wssd, wdil, wmem, wout, e4):
    m, d = x2d.shape
    tm = MERGE_ROWS
    row = lambda width, dil=1: pl.BlockSpec((tm // dil, width * dil), lambda i: (i, 0))
    dils = [dil for _, dil in DIL_PAIRS]
    ng = len(dils)
    return pl.pallas_call(
        _merge_kernel,
        grid=(m // tm,),
        in_specs=[row(d), row(d)] + [row(DIL_OUT_WIDTH, dil) for dil in dils]
                 + [row(LANES, dil) for dil in dils]
                 + [row(MEM_HEADS * MEM_HEAD_PAD), row(3 * d),
                    _resident(wssd.shape), _resident(wdil.shape), _resident(wmem.shape),
                    _resident(wout.shape), _resident(e4.shape)],
        out_specs=row(d),
        out_shape=jax.ShapeDtypeStruct((m, d), F32),
        scratch_shapes=[pltpu.VMEM((tm, LANES), F32)] * (3 * ng),
        compiler_params=pltpu.CompilerParams(vmem_limit_bytes=VMEM_LIMIT_BYTES),
        name="merge",
    )(x2d, yssd, o[0], o[1], o[2], lse[0], lse[1], lse[2], ymem, gate, wssd, wdil, wmem, wout, e4)


def _pad_heads(w, axis):
    shape = list(w.shape)
    shape[axis:axis + 1] = [MEM_HEADS, MEM_HEAD_DIM]
    w = w.reshape(shape)
    pad = [(0, 0)] * w.ndim
    pad[axis + 1] = (0, MEM_HEAD_PAD - MEM_HEAD_DIM)
    w = jnp.pad(w, pad)
    shape[axis:axis + 2] = [MEM_HEADS * MEM_HEAD_PAD]
    return w.reshape(shape)


def _mix_block(x, mem, positions, norm_mix_w, w_in, ssd_conv_w, ssd_conv_b, ssd_dt_bias,
               ssd_a_log, ssd_d, ssd_norm_w, dil_q_norm_w, dil_k_norm_w, mem_norm_w, w_mem_kv,
               mem_q_norm_w, mem_k_norm_w, w_ssd_br, w_dil_br, w_mem_br, w_out):
    b, s, d = x.shape
    m = b * s
    pad_lanes = lambda v: jnp.pad(v, (0, LANES - v.shape[0])).reshape(1, LANES)
    w_packed = jnp.concatenate([
        w_in[:, OFF_Z:OFF_DT],
        jnp.pad(w_in[:, OFF_DT:OFF_DQ], ((0, 0), (0, LANES - SSD_HEADS))),
        w_in[:, OFF_DQ:OFF_MQ],
        _pad_heads(w_in[:, OFF_MQ:OFF_GATE], 1),
        w_in[:, OFF_GATE:]], axis=1).astype(BF16)
    half = ROPE_DIMS // 2
    inv = jnp.exp(-math.log(ROPE_THETA) * (2.0 / ROPE_DIMS) * jnp.arange(half, dtype=F32))
    invp = jnp.tile(inv, LANES // half).reshape(1, LANES)
    pos_b = jnp.broadcast_to(positions.astype(F32).reshape(m, 1), (m, LANES))
    tile2 = lambda v: jnp.tile(v, 2).reshape(1, LANES)
    mqnw = jnp.pad(mem_q_norm_w, (0, MEM_HEAD_PAD - MEM_HEAD_DIM)).reshape(1, MEM_HEAD_PAD)
    mknw = jnp.pad(mem_k_norm_w, (0, MEM_HEAD_PAD - MEM_HEAD_DIM)).reshape(1, MEM_HEAD_PAD)
    z, xbc, dt_raw, q0, q1, q2, k0, k1, k2, v0, v1, v2, mq, gate = _in_proj(
        x.reshape(m, d), pos_b, norm_mix_w.reshape(1, d), w_packed,
        tile2(dil_q_norm_w), tile2(dil_k_norm_w), mqnw, invp)

    expand = jnp.repeat(jnp.eye(LANES, SSD_HEADS, dtype=BF16), SSD_HEAD_DIM, axis=1)
    y_ssd = _ssd(xbc.reshape(b, s, -1), z.reshape(b, s, -1), dt_raw.reshape(b, s, -1),
                 ssd_conv_w, ssd_conv_b.reshape(1, -1), pad_lanes(ssd_dt_bias), pad_lanes(ssd_a_log),
                 jnp.repeat(ssd_d, SSD_HEAD_DIM).reshape(1, d), ssd_norm_w.reshape(1, d), expand)

    outs, lses = [], []
    for gi, (window, dilation) in enumerate(DIL_PAIRS):
        assert window // dilation == DIL_BLOCK
        per_batch = lambda t: t.reshape(b, s // dilation, -1)
        o, lse = _dil_attn(per_batch((q0, q1, q2)[gi]), per_batch((k0, k1, k2)[gi]),
                           per_batch((v0, v1, v2)[gi]), gi, dilation)
        outs.append(o.reshape(m // dilation, -1))
        lses.append(lse.reshape(m // dilation, -1))

    w_kv = jnp.concatenate([_pad_heads(w_mem_kv[:, :MEM_WIDTH], 1),
                            _pad_heads(w_mem_kv[:, MEM_WIDTH:], 1)], axis=1).astype(BF16)
    mk, mv = _mem_kv(mem, mem_norm_w.reshape(1, d), w_kv, mknw)
    y_mem = _mem_attn(mq.reshape(b, s, -1), mk, mv).reshape(m, -1)

    e4 = jnp.repeat(jnp.eye(LANES, DIL_OUT_WIDTH // DIL_HEAD_DIM, dtype=F32), DIL_HEAD_DIM, axis=1)
    return _merge(x.reshape(m, d), y_ssd.reshape(m, d), outs, lses, y_mem, gate,
                  w_ssd_br.astype(BF16), w_dil_br.astype(BF16),
                  _pad_heads(w_mem_br, 0).astype(BF16), w_out.astype(BF16), e4)


PEER_TOKENS = 512
PEER_A_PER_STEP = 8
ROUTE_LANES = 128
ROUTE_TOKENS = 8 * ROUTE_LANES


def _oddeven_merge(lo, hi, r):
    step = r * 2
    if step < hi - lo:
        yield from _oddeven_merge(lo, hi, step)
        yield from _oddeven_merge(lo + r, hi, step)
        yield from [(i, i + r) for i in range(lo + r, hi - r, step)]
    else:
        yield (lo, lo + r)


def _oddeven_merge_sort(lo, hi):
    if hi - lo >= 1:
        mid = lo + (hi - lo) // 2
        yield from _oddeven_merge_sort(lo, mid)
        yield from _oddeven_merge_sort(mid + 1, hi)
        yield from _oddeven_merge(lo, hi, 1)


_SORT16 = tuple(_oddeven_merge_sort(0, PEER_TOPK - 1))


def _cmpx(a, i, j):
    hi = jnp.maximum(a[i], a[j])
    lo = jnp.minimum(a[i], a[j])
    a[i], a[j] = hi, lo


def _sort16_desc(a):
    for i, j in _SORT16:
        _cmpx(a, i, j)
    return a


def _bitonic16_desc(a):
    for d in (8, 4, 2, 1):
        for i in range(PEER_TOPK):
            if not i & d:
                _cmpx(a, i, i + d)
    return a


def _merge_top16(a, b):
    c = [jnp.maximum(a[k], b[PEER_TOPK - 1 - k]) for k in range(PEER_TOPK)]
    return _bitonic16_desc(c)


def _count_prefix(rows, test):
    def pick(bits, lo):
        if not bits:
            return rows[lo]
        (c, stride), rest = bits[0], bits[1:]
        return jnp.where(c, pick(rest, lo + stride), pick(rest, lo))

    bits = []
    for stride in (8, 4, 2, 1):
        bits.append((test(pick(bits, stride - 1)), stride))
    count = None
    for c, stride in bits:
        term = jnp.where(c, float(stride), 0.0)
        count = term if count is None else count + term
    return jnp.where(test(rows[PEER_TOPK - 1]), float(PEER_TOPK), count)


def _peer_route_kernel(h_ref, nw_ref, wqt_ref, keys_ref,
                       hnt_ref, cnt_ref, f1_ref, rank_ref, e2_ref, s1_ref, s2_ref, stage_ref):
    t = h_ref.shape[0]
    x = h_ref[...]
    ms = jnp.mean(x * x, axis=-1, keepdims=True)
    hn = x * lax.rsqrt(ms + EPS) * nw_ref[...]
    hnt = hn.T.astype(BF16)
    hnt_ref[...] = hnt
    ch = ROUTE_LANES
    nch = t // ch
    assert nch == 8, "one lane chunk per sublane of the token-major top-k arrays"
    neg = jnp.full((nch, ch), -jnp.inf, F32)

    def top16_tokens(s_ref):
        for c in range(nch):
            rows = [s_ref[8 * i:8 * i + 8, c * ch:(c + 1) * ch] for i in range(PEER_TOPK)]
            for i, v in enumerate(_sort16_desc(rows)):
                stage_ref[i * 64 + c * 8:i * 64 + c * 8 + 8, :] = v
        lists = [[stage_ref[pl.ds(i * 64 + r, nch, stride=8), :] for i in range(PEER_TOPK)]
                 for r in range(8)]
        while len(lists) > 1:
            lists = [_merge_top16(lists[k], lists[k + 1]) for k in range(0, len(lists), 2)]
        return lists[0]

    for h in range(PEER_HEADS):
        qt = jnp.dot(wqt_ref[h * PEER_QUERY_DIM:(h + 1) * PEER_QUERY_DIM, :], hnt,
                     preferred_element_type=F32).astype(BF16)
        half = PEER_QUERY_DIM // 2
        s1_ref[...] = jnp.dot(keys_ref[h, 0], qt[:half], preferred_element_type=F32)
        s2_ref[...] = jnp.dot(keys_ref[h, 1], qt[half:], preferred_element_type=F32)
        v1 = top16_tokens(s1_ref)
        v2 = top16_tokens(s2_ref)
        top = [v1[0] + v2[j] for j in range(PEER_TOPK)]
        for i in range(1, 8):
            n_i = PEER_TOPK // (i + 1)
            row = [v1[i] + v2[j] for j in range(n_i)] + [neg] * (PEER_TOPK - n_i)
            top = _merge_top16(top, row)
        tail = [v1[i] + v2[0] for i in range(8, PEER_TOPK)] + [neg] * 8
        tau = _merge_top16(top, tail)[PEER_TOPK - 1]
        e1 = [jnp.exp(v - v1[0]) for v in v1]
        e2 = [jnp.exp(v - v2[0]) for v in v2]
        z = jnp.zeros((nch, ch), F32)
        for i in range(PEER_TOPK):
            for j in range(PEER_TOPK):
                z = z + jnp.where(v1[i] + v2[j] >= tau, e1[i] * e2[j], 0.0)
        inv_z = 1.0 / z
        for c in range(nch):
            cols = slice(c * ch, (c + 1) * ch)
            s1 = s1_ref[:, cols]
            s2 = s2_ref[:, cols]
            tau_r = tau[c:c + 1, :]
            v2r = [v[c:c + 1, :] for v in v2]
            cnt = _count_prefix(v2r, lambda r: s1 + r >= tau_r)
            rank = _count_prefix(v2r, lambda r: r > s2)
            cnt_ref[h, :, cols] = cnt
            f1_ref[h, :, cols] = jnp.exp(s1 - v1[0][c:c + 1, :]) * inv_z[c:c + 1, :]
            rank_ref[h, :, cols] = rank.astype(BF16)
            e2_ref[h, :, cols] = jnp.exp(s2 - v2[0][c:c + 1, :]).astype(BF16)


def _peer_route(h2d, norm_w, wqt, keys):
    n, d = h2d.shape
    t = ROUTE_TOKENS
    route = lambda dt: jax.ShapeDtypeStruct((PEER_HEADS, PEER_N_KEYS, n), dt)
    route_spec = pl.BlockSpec((PEER_HEADS, PEER_N_KEYS, t), lambda i: (0, 0, i))
    return pl.pallas_call(
        _peer_route_kernel,
        grid=(n // t,),
        in_specs=[
            pl.BlockSpec((t, d), lambda i: (i, 0)),
            _resident((1, d)), _resident(wqt.shape), _resident(keys.shape),
        ],
        out_specs=[pl.BlockSpec((d, t), lambda i: (0, i)),
                   route_spec, route_spec, route_spec, route_spec],
        out_shape=[jax.ShapeDtypeStruct((d, n), BF16), route(F32), route(F32), route(BF16), route(BF16)],
        scratch_shapes=[pltpu.VMEM((PEER_N_KEYS, t), F32), pltpu.VMEM((PEER_N_KEYS, t), F32),
                        pltpu.VMEM((PEER_TOPK * 8 * (t // ROUTE_LANES), ROUTE_LANES), F32)],
        compiler_params=pltpu.CompilerParams(vmem_limit_bytes=VMEM_LIMIT_BYTES),
        name="peer_route",
    )(h2d, norm_w.reshape(1, d), wqt, keys)


BF16_ROWS = 16


def _peer_expert_kernel(h_ref, hnt_ref, cnt_ref, f1_ref, rank_ref, e2_ref,
                        down_ref, upt_ref, upt_last_ref, o_ref, acc_ref, coef_ref, w_ref,
                        cntb_ref, f1b_ref):
    step = pl.program_id(1)
    n_blocks = pl.num_programs(1)
    t = hnt_ref.shape[1]
    cur = step % 2

    @pl.when(step == 0)
    def _():
        acc_ref[...] = jnp.zeros_like(acc_ref)
        coef_ref[1] = jnp.zeros_like(coef_ref[1])

    for q in range(PEER_A_PER_STEP):
        a = step * PEER_A_PER_STEP + q
        for h in range(PEER_HEADS):
            i = q * PEER_HEADS + h
            cntb_ref[i] = jnp.broadcast_to(cnt_ref[h, pl.ds(a, 1), :], (BF16_ROWS, t)).astype(BF16)
            f1b_ref[i] = jnp.broadcast_to(f1_ref[h, pl.ds(a, 1), :], (BF16_ROWS, t)).astype(BF16)
    for c0 in range(0, t, 2 * LANES):
        cols = slice(c0, c0 + 2 * LANES)
        for r0 in range(0, PEER_N_KEYS, BF16_ROWS):
            rows = slice(r0, r0 + BF16_ROWS)
            ranks = [rank_ref[h, rows, cols] for h in range(PEER_HEADS)]
            e2s = [e2_ref[h, rows, cols] for h in range(PEER_HEADS)]
            for q in range(PEER_A_PER_STEP):
                w = None
                for h in range(PEER_HEADS):
                    i = q * PEER_HEADS + h
                    wh = jnp.where(ranks[h] < cntb_ref[i, :, cols], e2s[h],
                                   jnp.zeros_like(e2s[h])) * f1b_ref[i, :, cols]
                    w = wh if w is None else w + wh
                er = slice(q * PEER_N_KEYS + r0, q * PEER_N_KEYS + r0 + BF16_ROWS)
                w_ref[er, cols] = w

    acc_ref[...] += jnp.dot(upt_ref[...], coef_ref[1 - cur], preferred_element_type=F32)
    s = jnp.dot(down_ref[...], hnt_ref[...], preferred_element_type=F32)
    sb = s.astype(BF16)
    act = (0.5 * sb) * (1.0 + lax.erf(sb * (2.0 ** -0.5)))
    coef_ref[cur] = w_ref[...] * act

    @pl.when(step == n_blocks - 1)
    def _():
        last = jnp.dot(upt_last_ref[...], coef_ref[cur], preferred_element_type=F32)
        o_ref[...] = h_ref[...] + (acc_ref[...] + last).T


def _peer_expert(h2d, hnt, cnt, f1, rank, e2, down_bf, upt_bf):
    n, d = h2d.shape
    t = PEER_TOKENS
    eb = PEER_A_PER_STEP * PEER_N_KEYS
    pairs = PEER_A_PER_STEP * PEER_HEADS
    route_spec = pl.BlockSpec((PEER_HEADS, PEER_N_KEYS, t), lambda i, j: (0, 0, i))
    n_blocks = PEER_N_KEYS // PEER_A_PER_STEP
    return pl.pallas_call(
        _peer_expert_kernel,
        grid=(n // t, n_blocks),
        in_specs=[
            pl.BlockSpec((t, d), lambda i, j: (i, 0)),
            pl.BlockSpec((d, t), lambda i, j: (0, i)),
            route_spec, route_spec, route_spec, route_spec,
            pl.BlockSpec((eb, d), lambda i, j: (j, 0)),
            pl.BlockSpec((d, eb), lambda i, j: (0, jnp.maximum(j - 1, 0))),
            pl.BlockSpec((d, eb), lambda i, j: (0, n_blocks - 1)),
        ],
        out_specs=pl.BlockSpec((t, d), lambda i, j: (i, 0)),
        out_shape=jax.ShapeDtypeStruct((n, d), F32),
        scratch_shapes=[pltpu.VMEM((d, t), F32), pltpu.VMEM((2, eb, t), BF16), pltpu.VMEM((eb, t), BF16),
                        pltpu.VMEM((pairs, BF16_ROWS, t), BF16), pltpu.VMEM((pairs, BF16_ROWS, t), BF16)],
        compiler_params=pltpu.CompilerParams(
            dimension_semantics=("arbitrary", "arbitrary"),
            vmem_limit_bytes=VMEM_LIMIT_BYTES),
        name="peer_expert",
    )(h2d, hnt, cnt, f1, rank, e2, down_bf, upt_bf, upt_bf)


def _peer_block(h2d, norm_w, w_query, sub_keys, expert_down, expert_up):
    wqt = w_query.T.astype(BF16)
    hnt, cnt, f1, rank, e2 = _peer_route(h2d, norm_w, wqt, sub_keys.astype(BF16))
    return _peer_expert(h2d, hnt, cnt, f1, rank, e2,
                        expert_down.astype(BF16), expert_up.T.astype(BF16))


def kernel(x, mem, positions, norm_mix_w, w_in, ssd_conv_w, ssd_conv_b, ssd_dt_bias,
           ssd_a_log, ssd_d, ssd_norm_w, dil_q_norm_w, dil_k_norm_w, mem_norm_w, w_mem_kv,
           mem_q_norm_w, mem_k_norm_w, w_ssd_br, w_dil_br, w_mem_br, w_out, norm_ffn_w,
           peer_w_query, peer_sub_keys, peer_down, peer_up):
    b, s, d = x.shape
    assert norm_mix_w.shape[0] == 1, "single-layer block"
    h = _mix_block(x, mem, positions, norm_mix_w[0], w_in[0], ssd_conv_w[0], ssd_conv_b[0],
                   ssd_dt_bias[0], ssd_a_log[0], ssd_d[0], ssd_norm_w[0], dil_q_norm_w[0],
                   dil_k_norm_w[0], mem_norm_w[0], w_mem_kv[0], mem_q_norm_w[0], mem_k_norm_w[0],
                   w_ssd_br[0], w_dil_br[0], w_mem_br[0], w_out[0])
    h = _peer_block(h, norm_ffn_w[0], peer_w_query[0], peer_sub_keys[0], peer_down[0], peer_up[0])
    return h.reshape(b, s, d)
```

```python
import math

import jax
import jax.numpy as jnp
from jax import lax
from jax.experimental import pallas as pl
from jax.experimental.pallas import tpu as pltpu

D_MODEL = 1024
EPS = 1e-6
SSD_HEAD_DIM = 64
SSD_HEADS = 16
SSD_GROUPS = 4
SSD_HEADS_PER_GROUP = 4
SSD_STATE = 128
SSD_CONV = 4
SSD_CHUNK = 128
SSD_CONV_DIM = 2048
DIL_PAIRS = ((128, 1), (512, 4), (2048, 16))
DIL_HEAD_DIM = 64
DIL_WIDTH = 768
DIL_OUT_WIDTH = 256
MEM_HEADS = 4
MEM_HEAD_DIM = 192
MEM_HEAD_PAD = 256
MEM_WIDTH = 768
ROPE_THETA = 500000.0
ROPE_DIMS = 16
PEER_HEADS = 8
PEER_N_KEYS = 128
PEER_QUERY_DIM = 256
PEER_TOPK = 16
OFF_Z = 0
OFF_XBC = 1024
OFF_DT = 3072
OFF_DQ = 3088
OFF_DK = OFF_DQ + DIL_WIDTH
OFF_DV = OFF_DK + DIL_WIDTH
OFF_MQ = OFF_DV + DIL_WIDTH
OFF_GATE = OFF_MQ + MEM_WIDTH
F32 = jnp.float32
BF16 = jnp.bfloat16
HIGHEST = lax.Precision.HIGHEST
LANES = 128
VMEM_LIMIT_BYTES = 56 * 1024 * 1024

P_Z = 0
P_XBC = 1024
P_DT = 3072
P_Q = 3200
P_K = P_Q + DIL_WIDTH
P_V = P_K + DIL_WIDTH
P_MQ = P_V + DIL_WIDTH
P_GATE = P_MQ + MEM_HEADS * MEM_HEAD_PAD
P_WIDTH = P_GATE + 3 * D_MODEL


def _resident(shape):
    nd = len(shape)
    return pl.BlockSpec(shape, lambda *_: (0,) * nd, pipeline_mode=pl.Buffered(1))


def _sigmoid(x):
    return 0.5 * jnp.tanh(0.5 * x) + 0.5


def _dot(a, b):
    return jnp.dot(a, b, preferred_element_type=F32)


def _dot_nt(a, b):
    return lax.dot_general(a, b, (((1,), (1,)), ((), ())), preferred_element_type=F32)


def _dot_f32(a, b):
    return jnp.dot(a, b, precision=HIGHEST, preferred_element_type=F32)


IN_PROJ_ROWS = 512
PLAIN_CHUNK = 256


def _in_proj_kernel(x_ref, pos_ref, nw_ref, w_ref, qnw_ref, knw_ref, mqnw_ref, invp_ref,
                    z_ref, xbc_ref, dt_ref, q0_ref, q1_ref, q2_ref, k0_ref, k1_ref, k2_ref,
                    v0_ref, v1_ref, v2_ref, mq_ref, gate_ref, u_ref, stage_a_ref, stage_b_ref):
    x = x_ref[...]
    ms = jnp.mean(x * x, axis=-1, keepdims=True)
    u_ref[...] = (x * lax.rsqrt(ms + EPS) * nw_ref[...]).astype(BF16)

    def seg(off, width):
        return _dot(u_ref[...], w_ref[:, off:off + width])

    for j in range(0, 1024, PLAIN_CHUNK):
        z_ref[:, j:j + PLAIN_CHUNK] = seg(P_Z + j, PLAIN_CHUNK).astype(BF16)
    for j in range(0, SSD_CONV_DIM, PLAIN_CHUNK):
        xbc_ref[:, j:j + PLAIN_CHUNK] = seg(P_XBC + j, PLAIN_CHUNK).astype(BF16)
    dt_ref[...] = seg(P_DT, LANES)
    rows = x.shape[0]

    stages = (stage_a_ref, stage_b_ref)

    def put_group(gi, o_ref):
        dil = DIL_PAIRS[gi][1]
        for i, st in enumerate(stages):
            for r in range(dil):
                tok = pl.ds(r, rows // dil, stride=dil) if dil > 1 else slice(None)
                lo = r * DIL_OUT_WIDTH + i * LANES
                o_ref[:, lo:lo + LANES] = st[tok, :].astype(BF16)

    for gi, o_ref in enumerate((v0_ref, v1_ref, v2_ref)):
        vt = seg(P_V + gi * DIL_OUT_WIDTH, DIL_OUT_WIDTH)
        for i, st in enumerate(stages):
            st[...] = vt[:, i * LANES:(i + 1) * LANES]
        put_group(gi, o_ref)
    for j in range(0, 3 * D_MODEL, PLAIN_CHUNK):
        gate_ref[:, j:j + PLAIN_CHUNK] = seg(P_GATE + j, PLAIN_CHUNK).astype(BF16)

    ang = pos_ref[...] * invp_ref[...]
    cos = jnp.cos(ang)
    sin = jnp.sin(ang)
    lane = lax.broadcasted_iota(jnp.int32, (rows, LANES), 1)
    l64 = lane & (DIL_HEAD_DIM - 1)
    first = l64 < ROPE_DIMS // 2
    rotated = l64 < ROPE_DIMS
    cm = jnp.where(rotated, cos, 1.0)
    sm = jnp.where(rotated, jnp.where(first, -sin, sin), 0.0)
    left = lane < DIL_HEAD_DIM

    def qk_norm_rope(off, w_row_ref, o_refs):
        for gi, o_ref in enumerate(o_refs):
            t2 = seg(off + gi * DIL_OUT_WIDTH, DIL_OUT_WIDTH)
            for i in range(2):
                t = t2[:, i * LANES:(i + 1) * LANES]
                sq = t * t
                ssl = jnp.sum(jnp.where(left, sq, 0.0), axis=-1, keepdims=True)
                ssr = jnp.sum(jnp.where(left, 0.0, sq), axis=-1, keepdims=True)
                r = jnp.where(left, lax.rsqrt(ssl / DIL_HEAD_DIM + EPS),
                              lax.rsqrt(ssr / DIL_HEAD_DIM + EPS))
                tn = t * r * w_row_ref[...]
                rot = jnp.where(first, pltpu.roll(tn, LANES - ROPE_DIMS // 2, axis=1),
                                pltpu.roll(tn, ROPE_DIMS // 2, axis=1))
                stages[i][...] = tn * cm + rot * sm
            put_group(gi, o_ref)

    qk_norm_rope(P_Q, qnw_ref, (q0_ref, q1_ref, q2_ref))
    qk_norm_rope(P_K, knw_ref, (k0_ref, k1_ref, k2_ref))

    for h in range(MEM_HEADS):
        t2 = seg(P_MQ + h * MEM_HEAD_PAD, MEM_HEAD_PAD)
        ss = jnp.sum(t2 * t2, axis=-1, keepdims=True)
        mq_ref[:, h * MEM_HEAD_PAD:(h + 1) * MEM_HEAD_PAD] = (
            t2 * lax.rsqrt(ss / MEM_HEAD_DIM + EPS) * mqnw_ref[...]).astype(BF16)


def _in_proj(x2d, pos_b, norm_w, w_packed, qnw, knw, mqnw, invp):
    m, d = x2d.shape
    tm = IN_PROJ_ROWS
    row = lambda width, dil=1: pl.BlockSpec((tm // dil, width * dil), lambda i: (i, 0))
    dil_outs = [(DIL_OUT_WIDTH, BF16, dil) for _, dil in DIL_PAIRS]
    outs = ([(1024, BF16, 1), (SSD_CONV_DIM, BF16, 1), (LANES, F32, 1)] + dil_outs * 3
            + [(MEM_HEADS * MEM_HEAD_PAD, BF16, 1), (3 * D_MODEL, BF16, 1)])
    return pl.pallas_call(
        _in_proj_kernel,
        grid=(m // tm,),
        in_specs=[row(d), row(LANES), _resident((1, d)), _resident(w_packed.shape),
                  _resident((1, LANES)), _resident((1, LANES)), _resident((1, MEM_HEAD_PAD)),
                  _resident((1, LANES))],
        out_specs=[row(w, dil) for w, _, dil in outs],
        out_shape=[jax.ShapeDtypeStruct((m // dil, w * dil), dt) for w, dt, dil in outs],
        scratch_shapes=[pltpu.VMEM((tm, d), BF16), pltpu.VMEM((tm, LANES), F32),
                        pltpu.VMEM((tm, LANES), F32)],
        compiler_params=pltpu.CompilerParams(vmem_limit_bytes=VMEM_LIMIT_BYTES),
        name="in_proj",
    )(x2d, pos_b, norm_w, w_packed, qnw, knw, mqnw, invp)


def _ssd_kernel(xbc_ref, z_ref, dt_ref, cw_ref, cb_ref, dtb_ref, alog_ref, dfull_ref, nw_ref,
                e_ref, y_ref, tail_ref, state_ref):
    c = pl.program_id(1)
    L = SSD_CHUNK

    @pl.when(c == 0)
    def _():
        tail_ref[...] = jnp.zeros_like(tail_ref)
        state_ref[...] = jnp.zeros_like(state_ref)

    cur = xbc_ref[0].astype(F32)
    xp = jnp.concatenate([tail_ref[...], cur], axis=0)
    conv = cb_ref[...] + cw_ref[SSD_CONV - 1:SSD_CONV, :] * cur
    for k in range(SSD_CONV - 1):
        lo = 8 - (SSD_CONV - 1) + k
        conv = conv + cw_ref[k:k + 1, :] * xp[lo:lo + L]
    tail_ref[...] = cur[L - 8:L]
    xa = conv * _sigmoid(conv)
    xs = xa[:, :D_MODEL]
    gn = SSD_GROUPS * SSD_STATE
    bm = xa[:, D_MODEL:D_MODEL + gn].astype(BF16)
    cm = xa[:, D_MODEL + gn:].astype(BF16)

    pre = dt_ref[0] + dtb_ref[...]
    dt = jnp.maximum(pre, 0.0) + jnp.log(1.0 + jnp.exp(-jnp.abs(pre)))
    da = dt * (-jnp.exp(alog_ref[...]))
    ri = lax.broadcasted_iota(jnp.int32, (L, L), 0)
    ci = lax.broadcasted_iota(jnp.int32, (L, L), 1)
    tril = ri >= ci
    acum = _dot_f32(tril.astype(F32), da)
    acum_t = acum.T
    def expand(v):
        hi = v.astype(BF16)
        lo = (v - hi.astype(F32)).astype(BF16)
        return _dot(hi, e_ref[...]) + _dot(lo, e_ref[...])

    dt_full = expand(dt)
    ea_full = expand(jnp.exp(acum))
    dec_full = expand(jnp.exp(acum[L - 1:L, :] - acum))
    xdt = xs * dt_full
    xdt_b = xdt.astype(BF16)
    xdd_b = (xdt * dec_full).astype(BF16)
    gw = SSD_HEADS_PER_GROUP * SSD_HEAD_DIM
    head_of_lane = lax.broadcasted_iota(jnp.int32, (L, gw), 1) // SSD_HEAD_DIM
    ys = []
    for g in range(SSD_GROUPS):
        bg = bm[:, g * SSD_STATE:(g + 1) * SSD_STATE]
        cg = cm[:, g * SSD_STATE:(g + 1) * SSD_STATE]
        gs = slice(g * gw, (g + 1) * gw)
        cb = _dot_nt(cg, bg)
        st_old = state_ref[g]
        yg = _dot(cg, st_old.astype(BF16)) * ea_full[:, gs]
        ms = []
        for r in range(SSD_HEADS_PER_GROUP):
            h = g * SSD_HEADS_PER_GROUP + r
            seg = acum[:, h:h + 1] - acum_t[h:h + 1, :]
            lmat = jnp.exp(jnp.where(tril, seg, -jnp.inf))
            ms.append((cb * lmat).astype(BF16))
        yd = _dot(jnp.concatenate(ms, axis=0), xdt_b[:, gs])
        for r in range(SSD_HEADS_PER_GROUP):
            yg = yg + jnp.where(head_of_lane == r, yd[r * L:(r + 1) * L], 0.0)
        ys.append(yg)
        s_new = lax.dot_general(bg, xdd_b[:, gs], (((0,), (0,)), ((), ())),
                                preferred_element_type=F32)
        state_ref[g] = st_old * ea_full[L - 1:L, gs] + s_new
    y = jnp.concatenate(ys, axis=1) + dfull_ref[...] * xs
    zf = z_ref[0].astype(F32)
    y = y * (zf * _sigmoid(zf))
    outs = []
    for g in range(SSD_GROUPS):
        yg = y[:, g * gw:(g + 1) * gw]
        ss = jnp.sum(yg * yg, axis=-1, keepdims=True)
        outs.append(yg * lax.rsqrt(ss / gw + EPS))
    y_ref[0] = (jnp.concatenate(outs, axis=1) * nw_ref[...]).astype(BF16)


def _ssd(xbc, z, dt_raw, conv_w, conv_b, dt_bias, a_log, d_full, norm_w, expand):
    b, s, _ = xbc.shape
    L = SSD_CHUNK
    blk = lambda width: pl.BlockSpec((1, L, width), lambda i, c: (i, c, 0))
    return pl.pallas_call(
        _ssd_kernel,
        grid=(b, s // L),
        in_specs=[blk(SSD_CONV_DIM), blk(D_MODEL), blk(LANES),
                  _resident(conv_w.shape), _resident(conv_b.shape), _resident(dt_bias.shape),
                  _resident(a_log.shape), _resident(d_full.shape), _resident(norm_w.shape),
                  _resident(expand.shape)],
        out_specs=blk(D_MODEL),
        out_shape=jax.ShapeDtypeStruct((b, s, D_MODEL), BF16),
        scratch_shapes=[pltpu.VMEM((8, SSD_CONV_DIM), F32),
                        pltpu.VMEM((SSD_GROUPS, SSD_STATE, SSD_HEADS_PER_GROUP * SSD_HEAD_DIM), F32)],
        compiler_params=pltpu.CompilerParams(
            dimension_semantics=("arbitrary", "arbitrary"), vmem_limit_bytes=VMEM_LIMIT_BYTES),
        name="ssd",
    )(xbc, z, dt_raw, conv_w, conv_b, dt_bias, a_log, d_full, norm_w, expand)


DIL_BLOCK = 128
DIL_QROWS = 512


def _dil_attn_kernel(q_ref, kc_ref, kp_ref, vc_ref, vp_ref, o_ref, lse_ref):
    n = pl.program_id(2)
    blk = DIL_BLOCK
    w = DIL_OUT_WIDTH
    nh = w // DIL_HEAD_DIM
    head_of_lane = lax.broadcasted_iota(jnp.int32, (blk, w), 1) // DIL_HEAD_DIM
    lane = lax.broadcasted_iota(jnp.int32, (blk, LANES), 1)
    qi = lax.broadcasted_iota(jnp.int32, (nh * blk, 2 * blk), 0) % blk
    ki = lax.broadcasted_iota(jnp.int32, (nh * blk, 2 * blk), 1)
    band = (ki >= qi) & (ki <= qi + blk)
    first_valid = band & (ki + jnp.minimum(n, 1) * blk >= blk)
    for sb in range(DIL_QROWS // blk):
        rows = slice(sb * blk, (sb + 1) * blk)
        q = q_ref[0, rows, :]
        if sb == 0:
            kprev, vprev, valid = kp_ref[0], vp_ref[0], first_valid
        else:
            prev = slice((sb - 1) * blk, sb * blk)
            kprev, vprev, valid = kc_ref[0, prev, :], vc_ref[0, prev, :], band
        kk = jnp.concatenate([kprev, kc_ref[0, rows, :]], axis=0)
        vv = jnp.concatenate([vprev, vc_ref[0, rows, :]], axis=0)
        qs = jnp.concatenate([jnp.where(head_of_lane == h, q, jnp.zeros_like(q))
                              for h in range(nh)], axis=0)
        s = _dot_nt(qs, kk) * (DIL_HEAD_DIM ** -0.5)
        s = jnp.where(valid, s, -jnp.inf)
        m = jnp.max(s, axis=-1, keepdims=True)
        p = jnp.exp(s - m)
        l = jnp.sum(p, axis=-1, keepdims=True)
        on = _dot(p.astype(BF16), vv) / l
        lse = m + jnp.log(l)
        o_acc = jnp.zeros((blk, w), F32)
        lse_t = jnp.zeros((blk, LANES), F32)
        for h in range(nh):
            hr = slice(h * blk, (h + 1) * blk)
            o_acc = jnp.where(head_of_lane == h, on[hr], o_acc)
            lse_t = jnp.where(lane == h, lse[hr], lse_t)
        o_ref[0, rows, :] = o_acc.astype(BF16)
        lse_ref[0, rows, :] = lse_t


def _dil_attn(q3, k3, v3, gi, dilation):
    b, m, _ = q3.shape
    qr = DIL_QROWS
    sub = qr // DIL_BLOCK
    cur = pl.BlockSpec((1, qr, DIL_OUT_WIDTH), lambda i, r, n: (i, n, r))
    prev = pl.BlockSpec((1, DIL_BLOCK, DIL_OUT_WIDTH),
                        lambda i, r, n: (i, jnp.maximum(n * sub - 1, 0), r))
    return pl.pallas_call(
        _dil_attn_kernel,
        grid=(b, dilation, m // qr),
        in_specs=[cur, cur, prev, cur, prev],
        out_specs=[pl.BlockSpec((1, qr, DIL_OUT_WIDTH), lambda i, r, n: (i, n, r)),
                   pl.BlockSpec((1, qr, LANES), lambda i, r, n: (i, n, r))],
        out_shape=[jax.ShapeDtypeStruct((b, m, dilation * DIL_OUT_WIDTH), BF16),
                   jax.ShapeDtypeStruct((b, m, dilation * LANES), F32)],
        compiler_params=pltpu.CompilerParams(vmem_limit_bytes=VMEM_LIMIT_BYTES),
        name=f"dil_attn_g{gi}",
    )(q3, k3, k3, v3, v3)


def _mem_kv_kernel(mem_ref, nw_ref, w_ref, knw_ref, k_ref, v_ref):
    x = mem_ref[0]
    ms = jnp.mean(x * x, axis=-1, keepdims=True)
    u = (x * lax.rsqrt(ms + EPS) * nw_ref[...]).astype(BF16)
    width = MEM_HEADS * MEM_HEAD_PAD
    for h in range(MEM_HEADS):
        hs = slice(h * MEM_HEAD_PAD, (h + 1) * MEM_HEAD_PAD)
        kh = _dot(u, w_ref[:, hs])
        ss = jnp.sum(kh * kh, axis=-1, keepdims=True)
        k_ref[0, :, hs] = (kh * lax.rsqrt(ss / MEM_HEAD_DIM + EPS) * knw_ref[...]).astype(BF16)
        vs = slice(width + h * MEM_HEAD_PAD, width + (h + 1) * MEM_HEAD_PAD)
        v_ref[0, :, hs] = _dot(u, w_ref[:, vs]).astype(BF16)


def _mem_kv(mem, norm_w, w_kv, knw):
    b, n_mem, d = mem.shape
    width = MEM_HEADS * MEM_HEAD_PAD
    blk = pl.BlockSpec((1, n_mem, width), lambda i: (i, 0, 0))
    return pl.pallas_call(
        _mem_kv_kernel,
        grid=(b,),
        in_specs=[pl.BlockSpec((1, n_mem, d), lambda i: (i, 0, 0)), _resident((1, d)),
                  _resident(w_kv.shape), _resident((1, MEM_HEAD_PAD))],
        out_specs=[blk, blk],
        out_shape=[jax.ShapeDtypeStruct((b, n_mem, width), BF16)] * 2,
        compiler_params=pltpu.CompilerParams(vmem_limit_bytes=VMEM_LIMIT_BYTES),
        name="mem_kv",
    )(mem, norm_w, w_kv, knw)


MEM_QROWS = 512


def _mem_attn_kernel(q_ref, k_ref, v_ref, o_ref):
    for h in range(MEM_HEADS):
        hs = slice(h * MEM_HEAD_PAD, (h + 1) * MEM_HEAD_PAD)
        s = _dot_nt(q_ref[0, :, hs], k_ref[0, :, hs]) * (MEM_HEAD_DIM ** -0.5)
        m = jnp.max(s, axis=-1, keepdims=True)
        p = jnp.exp(s - m)
        l = jnp.sum(p, axis=-1, keepdims=True)
        o_ref[0, :, hs] = (_dot(p.astype(BF16), v_ref[0, :, hs]) / l).astype(BF16)


def _mem_attn(q, k, v):
    b, s, width = q.shape
    n_mem = k.shape[1]
    tm = MEM_QROWS
    kv = pl.BlockSpec((1, n_mem, width), lambda i, j: (i, 0, 0))
    return pl.pallas_call(
        _mem_attn_kernel,
        grid=(b, s // tm),
        in_specs=[pl.BlockSpec((1, tm, width), lambda i, j: (i, j, 0)), kv, kv],
        out_specs=pl.BlockSpec((1, tm, width), lambda i, j: (i, j, 0)),
        out_shape=jax.ShapeDtypeStruct((b, s, width), BF16),
        compiler_params=pltpu.CompilerParams(vmem_limit_bytes=VMEM_LIMIT_BYTES),
        name="mem_attn",
    )(q, k, v)


MERGE_ROWS = 512
MERGE_SLAB = 256


def _merge_kernel(x_ref, yssd_ref, o0_ref, o1_ref, o2_ref, l0_ref, l1_ref, l2_ref, ymem_ref,
                  gate_ref, wssd_ref, wdil_ref, wmem_ref, wout_ref, e4_ref, h_ref,
                  os0a_ref, os0b_ref, os1a_ref, os1b_ref, os2a_ref, os2b_ref,
                  ls0_ref, ls1_ref, ls2_ref):
    ostage = ((os0a_ref, os0b_ref), (os1a_ref, os1b_ref), (os2a_ref, os2b_ref))
    lstage = (ls0_ref, ls1_ref, ls2_ref)
    for gi, (o_ref, l_ref) in enumerate(((o0_ref, l0_ref), (o1_ref, l1_ref), (o2_ref, l2_ref))):
        dil = DIL_PAIRS[gi][1]
        n = MERGE_ROWS // dil
        for r in range(dil):
            tok = pl.ds(r, n, stride=dil) if dil > 1 else slice(None)
            for i in range(2):
                lo = r * DIL_OUT_WIDTH + i * LANES
                ostage[gi][i][tok, :] = o_ref[:, lo:lo + LANES].astype(F32)
            lstage[gi][tok, :] = l_ref[:, r * LANES:(r + 1) * LANES]
    for r0 in range(0, MERGE_ROWS, MERGE_SLAB):
        rows = slice(r0, r0 + MERGE_SLAB)
        l0, l1, l2 = ls0_ref[rows, :], ls1_ref[rows, :], ls2_ref[rows, :]
        lmax = jnp.maximum(jnp.maximum(l0, l1), l2)
        e0, e1, e2 = jnp.exp(l0 - lmax), jnp.exp(l1 - lmax), jnp.exp(l2 - lmax)
        inv = 1.0 / (e0 + e1 + e2)
        ydil = jnp.zeros((MERGE_SLAB, DIL_OUT_WIDTH), F32)
        wts = jnp.concatenate([e0 * inv, e1 * inv, e2 * inv], axis=0)
        hi = wts.astype(BF16)
        lo = (wts - hi.astype(F32)).astype(BF16)
        full = _dot(jnp.concatenate([hi, lo], axis=0), e4_ref[...])
        ng = len(ostage)
        for gi in range(ng):
            og = jnp.concatenate([ostage[gi][0][rows, :], ostage[gi][1][rows, :]], axis=1)
            wg = (full[gi * MERGE_SLAB:(gi + 1) * MERGE_SLAB]
                  + full[(ng + gi) * MERGE_SLAB:(ng + gi + 1) * MERGE_SLAB])
            ydil = ydil + wg * og
        merged = (_sigmoid(gate_ref[rows, 0:D_MODEL].astype(F32))
                  * _dot(yssd_ref[rows, :], wssd_ref[...])
                  + _sigmoid(gate_ref[rows, D_MODEL:2 * D_MODEL].astype(F32))
                  * _dot(ydil.astype(BF16), wdil_ref[...])
                  + _sigmoid(gate_ref[rows, 2 * D_MODEL:3 * D_MODEL].astype(F32))
                  * _dot(ymem_ref[rows, :], wmem_ref[...]))
        h_ref[rows, :] = x_ref[rows, :] + _dot(merged.astype(BF16), wout_ref[...])


def _merge(x2d, yssd, o, lse, ymem, gate, wssd, wdil, wmem, wout, e4):
    m, d = x2d.shape
    tm = MERGE_ROWS
    row = lambda width, dil=1: pl.BlockSpec((tm // dil, width * dil), lambda i: (i, 0))
    dils = [dil for _, dil in DIL_PAIRS]
    ng = len(dils)
    return pl.pallas_call(
        _merge_kernel,
        grid=(m // tm,),
        in_specs=[row(d), row(d)] + [row(DIL_OUT_WIDTH, dil) for dil in dils]
                 + [row(LANES, dil) for dil in dils]
                 + [row(MEM_HEADS * MEM_HEAD_PAD), row(3 * d),
                    _resident(wssd.shape), _resident(wdil.shape), _resident(wmem.shape),
                    _resident(wout.shape), _resident(e4.shape)],
        out_specs=row(d),
        out_shape=jax.ShapeDtypeStruct((m, d), F32),
        scratch_shapes=[pltpu.VMEM((tm, LANES), F32)] * (3 * ng),
        compiler_params=pltpu.CompilerParams(vmem_limit_bytes=VMEM_LIMIT_BYTES),
        name="merge",
    )(x2d, yssd, o[0], o[1], o[2], lse[0], lse[1], lse[2], ymem, gate, wssd, wdil, wmem, wout, e4)


def _pad_heads(w, axis):
    shape = list(w.shape)
    shape[axis:axis + 1] = [MEM_HEADS, MEM_HEAD_DIM]
    w = w.reshape(shape)
    pad = [(0, 0)] * w.ndim
    pad[axis + 1] = (0, MEM_HEAD_PAD - MEM_HEAD_DIM)
    w = jnp.pad(w, pad)
    shape[axis:axis + 2] = [MEM_HEADS * MEM_HEAD_PAD]
    return w.reshape(shape)


def _mix_block(x, mem, positions, norm_mix_w, w_in, ssd_conv_w, ssd_conv_b, ssd_dt_bias,
               ssd_a_log, ssd_d, ssd_norm_w, dil_q_norm_w, dil_k_norm_w, mem_norm_w, w_mem_kv,
               mem_q_norm_w, mem_k_norm_w, w_ssd_br, w_dil_br, w_mem_br, w_out):
    b, s, d = x.shape
    m = b * s
    pad_lanes = lambda v: jnp.pad(v, (0, LANES - v.shape[0])).reshape(1, LANES)
    w_packed = jnp.concatenate([
        w_in[:, OFF_Z:OFF_DT],
        jnp.pad(w_in[:, OFF_DT:OFF_DQ], ((0, 0), (0, LANES - SSD_HEADS))),
        w_in[:, OFF_DQ:OFF_MQ],
        _pad_heads(w_in[:, OFF_MQ:OFF_GATE], 1),
        w_in[:, OFF_GATE:]], axis=1).astype(BF16)
    half = ROPE_DIMS // 2
    inv = jnp.exp(-math.log(ROPE_THETA) * (2.0 / ROPE_DIMS) * jnp.arange(half, dtype=F32))
    invp = jnp.tile(inv, LANES // half).reshape(1, LANES)
    pos_b = jnp.broadcast_to(positions.astype(F32).reshape(m, 1), (m, LANES))
    tile2 = lambda v: jnp.tile(v, 2).reshape(1, LANES)
    mqnw = jnp.pad(mem_q_norm_w, (0, MEM_HEAD_PAD - MEM_HEAD_DIM)).reshape(1, MEM_HEAD_PAD)
    mknw = jnp.pad(mem_k_norm_w, (0, MEM_HEAD_PAD - MEM_HEAD_DIM)).reshape(1, MEM_HEAD_PAD)
    z, xbc, dt_raw, q0, q1, q2, k0, k1, k2, v0, v1, v2, mq, gate = _in_proj(
        x.reshape(m, d), pos_b, norm_mix_w.reshape(1, d), w_packed,
        tile2(dil_q_norm_w), tile2(dil_k_norm_w), mqnw, invp)

    expand = jnp.repeat(jnp.eye(LANES, SSD_HEADS, dtype=BF16), SSD_HEAD_DIM, axis=1)
    y_ssd = _ssd(xbc.reshape(b, s, -1), z.reshape(b, s, -1), dt_raw.reshape(b, s, -1),
                 ssd_conv_w, ssd_conv_b.reshape(1, -1), pad_lanes(ssd_dt_bias), pad_lanes(ssd_a_log),
                 jnp.repeat(ssd_d, SSD_HEAD_DIM).reshape(1, d), ssd_norm_w.reshape(1, d), expand)

    outs, lses = [], []
    for gi, (window, dilation) in enumerate(DIL_PAIRS):
        assert window // dilation == DIL_BLOCK
        per_batch = lambda t: t.reshape(b, s // dilation, -1)
        o, lse = _dil_attn(per_batch((q0, q1, q2)[gi]), per_batch((k0, k1, k2)[gi]),
                           per_batch((v0, v1, v2)[gi]), gi, dilation)
        outs.append(o.reshape(m // dilation, -1))
        lses.append(lse.reshape(m // dilation, -1))

    w_kv = jnp.concatenate([_pad_heads(w_mem_kv[:, :MEM_WIDTH], 1),
                            _pad_heads(w_mem_kv[:, MEM_WIDTH:], 1)], axis=1).astype(BF16)
    mk, mv = _mem_kv(mem, mem_norm_w.reshape(1, d), w_kv, mknw)
    y_mem = _mem_attn(mq.reshape(b, s, -1), mk, mv).reshape(m, -1)

    e4 = jnp.repeat(jnp.eye(LANES, DIL_OUT_WIDTH // DIL_HEAD_DIM, dtype=BF16), DIL_HEAD_DIM, axis=1)
    return _merge(x.reshape(m, d), y_ssd.reshape(m, d), outs, lses, y_mem, gate,
                  w_ssd_br.astype(BF16), w_dil_br.astype(BF16),
                  _pad_heads(w_mem_br, 0).astype(BF16), w_out.astype(BF16), e4)


PEER_TOKENS = 512
PEER_A_PER_STEP = 8
ROUTE_LANES = 128
ROUTE_TOKENS = 8 * ROUTE_LANES


def _oddeven_merge(lo, hi, r):
    step = r * 2
    if step < hi - lo:
        yield from _oddeven_merge(lo, hi, step)
        yield from _oddeven_merge(lo + r, hi, step)
        yield from [(i, i + r) for i in range(lo + r, hi - r, step)]
    else:
        yield (lo, lo + r)


def _oddeven_merge_sort(lo, hi):
    if hi - lo >= 1:
        mid = lo + (hi - lo) // 2
        yield from _oddeven_merge_sort(lo, mid)
        yield from _oddeven_merge_sort(mid + 1, hi)
        yield from _oddeven_merge(lo, hi, 1)


_SORT16 = tuple(_oddeven_merge_sort(0, PEER_TOPK - 1))


def _cmpx(a, i, j):
    hi = jnp.maximum(a[i], a[j])
    lo = jnp.minimum(a[i], a[j])
    a[i], a[j] = hi, lo


def _sort16_desc(a):
    for i, j in _SORT16:
        _cmpx(a, i, j)
    return a


def _bitonic16_desc(a):
    for d in (8, 4, 2, 1):
        for i in range(PEER_TOPK):
            if not i & d:
                _cmpx(a, i, i + d)
    return a


def _merge_top16(a, b):
    c = [jnp.maximum(a[k], b[PEER_TOPK - 1 - k]) for k in range(PEER_TOPK)]
    return _bitonic16_desc(c)


def _count_prefix(rows, test):
    def pick(bits, lo):
        if not bits:
            return rows[lo]
        (c, stride), rest = bits[0], bits[1:]
        return jnp.where(c, pick(rest, lo + stride), pick(rest, lo))

    bits = []
    for stride in (8, 4, 2, 1):
        bits.append((test(pick(bits, stride - 1)), stride))
    count = None
    for c, stride in bits:
        term = jnp.where(c, float(stride), 0.0)
        count = term if count is None else count + term
    return jnp.where(test(rows[PEER_TOPK - 1]), float(PEER_TOPK), count)


def _peer_route_kernel(h_ref, nw_ref, wqt_ref, keys_ref,
                       hnt_ref, cnt_ref, f1_ref, rank_ref, e2_ref, s1_ref, s2_ref, stage_ref):
    t = h_ref.shape[0]
    x = h_ref[...]
    ms = jnp.mean(x * x, axis=-1, keepdims=True)
    hn = x * lax.rsqrt(ms + EPS) * nw_ref[...]
    hnt = hn.T.astype(BF16)
    hnt_ref[...] = hnt
    ch = ROUTE_LANES
    nch = t // ch
    assert nch == 8, "one lane chunk per sublane of the token-major top-k arrays"
    neg = jnp.full((nch, ch), -jnp.inf, F32)

    def top16_tokens(s_ref):
        for c in range(nch):
            rows = [s_ref[8 * i:8 * i + 8, c * ch:(c + 1) * ch] for i in range(PEER_TOPK)]
            for i, v in enumerate(_sort16_desc(rows)):
                stage_ref[i * 64 + c * 8:i * 64 + c * 8 + 8, :] = v
        lists = [[stage_ref[pl.ds(i * 64 + r, nch, stride=8), :] for i in range(PEER_TOPK)]
                 for r in range(8)]
        while len(lists) > 1:
            lists = [_merge_top16(lists[k], lists[k + 1]) for k in range(0, len(lists), 2)]
        return lists[0]

    for h in range(PEER_HEADS):
        qt = jnp.dot(wqt_ref[h * PEER_QUERY_DIM:(h + 1) * PEER_QUERY_DIM, :], hnt,
                     preferred_element_type=F32).astype(BF16)
        half = PEER_QUERY_DIM // 2
        s1_ref[...] = jnp.dot(keys_ref[h, 0], qt[:half], preferred_element_type=F32)
        s2_ref[...] = jnp.dot(keys_ref[h, 1], qt[half:], preferred_element_type=F32)
        v1 = top16_tokens(s1_ref)
        v2 = top16_tokens(s2_ref)
        top = [v1[0] + v2[j] for j in range(PEER_TOPK)]
        for i in range(1, 8):
            n_i = PEER_TOPK // (i + 1)
            row = [v1[i] + v2[j] for j in range(n_i)] + [neg] * (PEER_TOPK - n_i)
            top = _merge_top16(top, row)
        tail = [v1[i] + v2[0] for i in range(8, PEER_TOPK)] + [neg] * 8
        tau = _merge_top16(top, tail)[PEER_TOPK - 1]
        e1 = [jnp.exp(v - v1[0]) for v in v1]
        e2 = [jnp.exp(v - v2[0]) for v in v2]
        z = jnp.zeros((nch, ch), F32)
        for i in range(PEER_TOPK):
            for j in range(PEER_TOPK):
                z = z + jnp.where(v1[i] + v2[j] >= tau, e1[i] * e2[j], 0.0)
        inv_z = 1.0 / z
        for c in range(nch):
            cols = slice(c * ch, (c + 1) * ch)
            s1 = s1_ref[:, cols]
            s2 = s2_ref[:, cols]
            tau_r = tau[c:c + 1, :]
            v2r = [v[c:c + 1, :] for v in v2]
            cnt = _count_prefix(v2r, lambda r: s1 + r >= tau_r)
            rank = _count_prefix(v2r, lambda r: r > s2)
            cnt_ref[h, :, cols] = cnt
            f1_ref[h, :, cols] = jnp.exp(s1 - v1[0][c:c + 1, :]) * inv_z[c:c + 1, :]
            rank_ref[h, :, cols] = rank.astype(BF16)
            e2_ref[h, :, cols] = jnp.exp(s2 - v2[0][c:c + 1, :]).astype(BF16)


def _peer_route(h2d, norm_w, wqt, keys):
    n, d = h2d.shape
    t = ROUTE_TOKENS
    route = lambda dt: jax.ShapeDtypeStruct((PEER_HEADS, PEER_N_KEYS, n), dt)
    route_spec = pl.BlockSpec((PEER_HEADS, PEER_N_KEYS, t), lambda i: (0, 0, i))
    return pl.pallas_call(
        _peer_route_kernel,
        grid=(n // t,),
        in_specs=[
            pl.BlockSpec((t, d), lambda i: (i, 0)),
            _resident((1, d)), _resident(wqt.shape), _resident(keys.shape),
        ],
        out_specs=[pl.BlockSpec((d, t), lambda i: (0, i)),
                   route_spec, route_spec, route_spec, route_spec],
        out_shape=[jax.ShapeDtypeStruct((d, n), BF16), route(F32), route(F32), route(BF16), route(BF16)],
        scratch_shapes=[pltpu.VMEM((PEER_N_KEYS, t), F32), pltpu.VMEM((PEER_N_KEYS, t), F32),
                        pltpu.VMEM((PEER_TOPK * 8 * (t // ROUTE_LANES), ROUTE_LANES), F32)],
        compiler_params=pltpu.CompilerParams(vmem_limit_bytes=VMEM_LIMIT_BYTES),
        name="peer_route",
    )(h2d, norm_w.reshape(1, d), wqt, keys)


BF16_ROWS = 16


def _peer_expert_kernel(h_ref, hnt_ref, cnt_ref, f1_ref, rank_ref, e2_ref,
                        down_ref, upt_ref, upt_last_ref, o_ref, acc_ref, coef_ref, w_ref,
                        cntb_ref, f1b_ref):
    step = pl.program_id(1)
    n_blocks = pl.num_programs(1)
    t = hnt_ref.shape[1]
    cur = step % 2

    @pl.when(step == 0)
    def _():
        acc_ref[...] = jnp.zeros_like(acc_ref)
        coef_ref[1] = jnp.zeros_like(coef_ref[1])

    for q in range(PEER_A_PER_STEP):
        a = step * PEER_A_PER_STEP + q
        for h in range(PEER_HEADS):
            i = q * PEER_HEADS + h
            cntb_ref[i] = jnp.broadcast_to(cnt_ref[h, pl.ds(a, 1), :], (BF16_ROWS, t)).astype(BF16)
            f1b_ref[i] = jnp.broadcast_to(f1_ref[h, pl.ds(a, 1), :], (BF16_ROWS, t)).astype(BF16)
    for c0 in range(0, t, 2 * LANES):
        cols = slice(c0, c0 + 2 * LANES)
        for r0 in range(0, PEER_N_KEYS, BF16_ROWS):
            rows = slice(r0, r0 + BF16_ROWS)
            ranks = [rank_ref[h, rows, cols] for h in range(PEER_HEADS)]
            e2s = [e2_ref[h, rows, cols] for h in range(PEER_HEADS)]
            for q in range(PEER_A_PER_STEP):
                w = None
                for h in range(PEER_HEADS):
                    i = q * PEER_HEADS + h
                    wh = jnp.where(ranks[h] < cntb_ref[i, :, cols], e2s[h],
                                   jnp.zeros_like(e2s[h])) * f1b_ref[i, :, cols]
                    w = wh if w is None else w + wh
                er = slice(q * PEER_N_KEYS + r0, q * PEER_N_KEYS + r0 + BF16_ROWS)
                w_ref[er, cols] = w

    acc_ref[...] += jnp.dot(upt_ref[...], coef_ref[1 - cur], preferred_element_type=F32)
    s = jnp.dot(down_ref[...], hnt_ref[...], preferred_element_type=F32)
    sb = s.astype(BF16)
    act = (0.5 * sb) * (1.0 + lax.erf(sb * (2.0 ** -0.5)))
    coef_ref[cur] = w_ref[...] * act

    @pl.when(step == n_blocks - 1)
    def _():
        last = jnp.dot(upt_last_ref[...], coef_ref[cur], preferred_element_type=F32)
        o_ref[...] = h_ref[...] + (acc_ref[...] + last).T


def _peer_expert(h2d, hnt, cnt, f1, rank, e2, down_bf, upt_bf):
    n, d = h2d.shape
    t = PEER_TOKENS
    eb = PEER_A_PER_STEP * PEER_N_KEYS
    pairs = PEER_A_PER_STEP * PEER_HEADS
    route_spec = pl.BlockSpec((PEER_HEADS, PEER_N_KEYS, t), lambda i, j: (0, 0, i))
    n_blocks = PEER_N_KEYS // PEER_A_PER_STEP
    return pl.pallas_call(
        _peer_expert_kernel,
        grid=(n // t, n_blocks),
        in_specs=[
            pl.BlockSpec((t, d), lambda i, j: (i, 0)),
            pl.BlockSpec((d, t), lambda i, j: (0, i)),
            route_spec, route_spec, route_spec, route_spec,
            pl.BlockSpec((eb, d), lambda i, j: (j, 0)),
            pl.BlockSpec((d, eb), lambda i, j: (0, jnp.maximum(j - 1, 0))),
            pl.BlockSpec((d, eb), lambda i, j: (0, n_blocks - 1)),
        ],
        out_specs=pl.BlockSpec((t, d), lambda i, j: (i, 0)),
        out_shape=jax.ShapeDtypeStruct((n, d), F32),
        scratch_shapes=[pltpu.VMEM((d, t), F32), pltpu.VMEM((2, eb, t), BF16), pltpu.VMEM((eb, t), BF16),
                        pltpu.VMEM((pairs, BF16_ROWS, t), BF16), pltpu.VMEM((pairs, BF16_ROWS, t), BF16)],
        compiler_params=pltpu.CompilerParams(
            dimension_semantics=("arbitrary", "arbitrary"),
            vmem_limit_bytes=VMEM_LIMIT_BYTES),
        name="peer_expert",
    )(h2d, hnt, cnt, f1, rank, e2, down_bf, upt_bf, upt_bf)


def _peer_block(h2d, norm_w, w_query, sub_keys, expert_down, expert_up):
    wqt = w_query.T.astype(BF16)
    hnt, cnt, f1, rank, e2 = _peer_route(h2d, norm_w, wqt, sub_keys.astype(BF16))
    return _peer_expert(h2d, hnt, cnt, f1, rank, e2,
                        expert_down.astype(BF16), expert_up.T.astype(BF16))


def kernel(x, mem, positions, norm_mix_w, w_in, ssd_conv_w, ssd_conv_b, ssd_dt_bias,
           ssd_a_log, ssd_d, ssd_norm_w, dil_q_norm_w, dil_k_norm_w, mem_norm_w, w_mem_kv,
           mem_q_norm_w, mem_k_norm_w, w_ssd_br, w_dil_br, w_mem_br, w_out, norm_ffn_w,
           peer_w_query, peer_sub_keys, peer_down, peer_up):
    b, s, d = x.shape
    assert norm_mix_w.shape[0] == 1, "single-layer block"
    h = _mix_block(x, mem, positions, norm_mix_w[0], w_in[0], ssd_conv_w[0], ssd_conv_b[0],
                   ssd_dt_bias[0], ssd_a_log[0], ssd_d[0], ssd_norm_w[0], dil_q_norm_w[0],
                   dil_k_norm_w[0], mem_norm_w[0], w_mem_kv[0], mem_q_norm_w[0], mem_k_norm_w[0],
                   w_ssd_br[0], w_dil_br[0], w_mem_br[0], w_out[0])
    h = _peer_block(h, norm_ffn_w[0], peer_w_query[0], peer_sub_keys[0], peer_down[0], peer_up[0])
    return h.reshape(b, s, d)
```
